```python
import jax, jax.numpy as jnp
from jax import lax
import numpy as np

D_MODEL = 1024
BATCH = 4
SEQ = 4096
DEPTH = 1
DEC_BATCH = 16
DEC_SEQ = 4096
PAST_LEN = 128

HEAD_DIM = 64
N_HEADS = 8
N_KV_HEADS = 2
Q_PER_KV = N_HEADS // N_KV_HEADS
ATTN_WIDTH = N_HEADS * HEAD_DIM
KV_WIDTH = N_KV_HEADS * HEAD_DIM
N_FGROUPS = 8
FGROUP_DIM = 64
F_WIDTH = N_FGROUPS * FGROUP_DIM
MIX_WIDTH = ATTN_WIDTH + F_WIDTH
IN_WIDTH = ATTN_WIDTH + 2 * KV_WIDTH + ATTN_WIDTH + F_WIDTH + F_WIDTH
PLE_DIM = 256
GRID_W = 64
Q_BLOCK = 128
ROPE_THETA = 10000.0
EPS = 1e-6

kernel_name = "hybrid_gqa_fnet_encoder"


def _rmsnorm(x, g):
    xf = x.astype(jnp.float32)
    xf = xf * lax.rsqrt(jnp.mean(xf * xf, axis=-1, keepdims=True) + EPS)
    return (xf * g.astype(jnp.float32)).astype(x.dtype)


def _axial_rope_tables(seq_len, dtype):
    rows = seq_len // GRID_W
    row = jnp.repeat(jnp.arange(rows, dtype=jnp.float32), GRID_W)
    col = jnp.tile(jnp.arange(GRID_W, dtype=jnp.float32), rows)
    half = HEAD_DIM // 2
    n_freq = half // 2
    inv_freq = ROPE_THETA ** (-jnp.arange(n_freq, dtype=jnp.float32) / n_freq)
    ang_r = row[:, None] * inv_freq[None, :]
    ang_c = col[:, None] * inv_freq[None, :]
    ang = jnp.concatenate([ang_r, ang_r, ang_c, ang_c], axis=-1)
    return jnp.cos(ang).astype(dtype), jnp.sin(ang).astype(dtype)


def _apply_axial_rope(x, cos, sin):
    x1, x2, x3, x4 = jnp.split(x, 4, axis=-1)
    rot = jnp.concatenate([-x2, x1, -x4, x3], axis=-1)
    return x * cos[None, :, None, :] + rot * sin[None, :, None, :]


def _blocked_gqa(q, k, v):
    B, S, _, _ = q.shape
    n_blk = S // Q_BLOCK
    scale = HEAD_DIM ** -0.5
    qb = q.reshape(B, n_blk, Q_BLOCK, N_KV_HEADS, Q_PER_KV, HEAD_DIM).transpose(1, 0, 3, 4, 2, 5)

    def one_block(qblk):
        s = jnp.einsum('bkgqd,bskd->bkgqs', qblk, k).astype(jnp.float32) * scale
        pr = jax.nn.softmax(s, axis=-1).astype(v.dtype)
        return jnp.einsum('bkgqs,bskd->bqkgd', pr, v)

    o = lax.map(one_block, qb)
    return o.transpose(1, 0, 2, 3, 4, 5).reshape(B, S, ATTN_WIDTH)


def _fourier_branch(f, w_fmix):
    B, S, _ = f.shape
    fg = f.reshape(B, S, N_FGROUPS, FGROUP_DIM).astype(jnp.float32)
    ff = jnp.real(jnp.fft.fft2(fg, axes=(1, 3), norm="ortho")).astype(f.dtype)
    fm = jnp.einsum('bsgc,gcd->bsgd', ff, w_fmix)
    return fm.reshape(B, S, F_WIDTH)


def _layer(h, p, g_norm, w_in, g_q, g_k, w_fmix, w_out, g_ple, w_ple_gate, w_ple):
    B, S, _ = h.shape
    u = _rmsnorm(h, g_norm)
    z = u @ w_in
    cuts = np.cumsum([ATTN_WIDTH, KV_WIDTH, KV_WIDTH, ATTN_WIDTH, F_WIDTH])
    q, k, v, ga, f, gf = jnp.split(z, [int(c) for c in cuts], axis=-1)
    q = q.reshape(B, S, N_HEADS, HEAD_DIM)
    k = k.reshape(B, S, N_KV_HEADS, HEAD_DIM)
    v = v.reshape(B, S, N_KV_HEADS, HEAD_DIM)
    cos, sin = _axial_rope_tables(S, h.dtype)
    q = _apply_axial_rope(_rmsnorm(q, g_q), cos, sin)
    k = _apply_axial_rope(_rmsnorm(k, g_k), cos, sin)
    a = _blocked_gqa(q, k, v) * jax.nn.silu(ga)
    fo = _fourier_branch(f, w_fmix) * jax.nn.silu(gf)
    h = h + jnp.concatenate([a, fo], axis=-1) @ w_out
    gate = jax.nn.sigmoid(_rmsnorm(h, g_ple) @ w_ple_gate)
    return h + (p @ w_ple) * gate


def _trunk(x, p, g_norm, w_in, g_q, g_k, w_fmix, w_out, g_ple, w_ple_gate, w_ple, g_final):
    h = x
    for i in range(DEPTH):
        h = _layer(h, p[i], g_norm[i], w_in[i], g_q[i], g_k[i], w_fmix[i], w_out[i],
                   g_ple[i], w_ple_gate[i], w_ple[i])
    return _rmsnorm(h, g_final)


def setup_inputs(seed: int = 0) -> dict:
    key = jax.random.key(seed)
    ks = jax.random.split(key, 16)
    f32 = jnp.float32
    nrm = lambda k, shape, s: jax.random.normal(k, shape, f32) * s
    return {
        "x_prompt": nrm(ks[0], (BATCH, SEQ, D_MODEL), 1.0),
        "x_sample": nrm(ks[1], (DEC_BATCH, DEC_SEQ, D_MODEL), 1.0),
        "p_prompt": nrm(ks[2], (DEPTH, BATCH, SEQ, PLE_DIM), 1.0),
        "p_sample": nrm(ks[3], (DEPTH, DEC_BATCH, DEC_SEQ, PLE_DIM), 1.0),
        "g_norm": 1.0 + nrm(ks[4], (DEPTH, D_MODEL), 0.02),
        "w_in": nrm(ks[5], (DEPTH, D_MODEL, IN_WIDTH), D_MODEL ** -0.5),
        "g_q": 1.0 + nrm(ks[6], (DEPTH, HEAD_DIM), 0.02),
        "g_k": 1.0 + nrm(ks[7], (DEPTH, HEAD_DIM), 0.02),
        "w_fmix": nrm(ks[8], (DEPTH, N_FGROUPS, FGROUP_DIM, FGROUP_DIM), FGROUP_DIM ** -0.5),
        "w_out": nrm(ks[9], (DEPTH, MIX_WIDTH, D_MODEL), MIX_WIDTH ** -0.5),
        "g_ple": 1.0 + nrm(ks[10], (DEPTH, D_MODEL), 0.02),
        "w_ple_gate": nrm(ks[11], (DEPTH, D_MODEL, D_MODEL), D_MODEL ** -0.5),
        "w_ple": nrm(ks[12], (DEPTH, PLE_DIM, D_MODEL), PLE_DIM ** -0.5),
        "g_final": 1.0 + nrm(ks[13], (D_MODEL,), 0.02),
    }


def reference(x_prompt, x_sample, p_prompt, p_sample, g_norm, w_in, g_q, g_k, w_fmix, w_out,
              g_ple, w_ple_gate, w_ple, g_final):
    y_prompt = _trunk(x_prompt, p_prompt, g_norm, w_in, g_q, g_k, w_fmix, w_out,
                      g_ple, w_ple_gate, w_ple, g_final)
    y_sample = _trunk(x_sample, p_sample, g_norm, w_in, g_q, g_k, w_fmix, w_out,
                      g_ple, w_ple_gate, w_ple, g_final)
    return (y_prompt, y_sample)
```

```python
import functools

import jax
import jax.numpy as jnp
import numpy as np
from jax import lax
from jax.experimental import pallas as pl
from jax.experimental.pallas import tpu as pltpu

HEAD_DIM = 64
N_HEADS = 8
N_KV_HEADS = 2
Q_PER_KV = N_HEADS // N_KV_HEADS
ATTN_WIDTH = N_HEADS * HEAD_DIM
KV_WIDTH = N_KV_HEADS * HEAD_DIM
N_FGROUPS = 8
FGROUP_DIM = 64
F_WIDTH = N_FGROUPS * FGROUP_DIM
GRID_W = 64
ROPE_THETA = 10000.0
EPS = 1e-6

LANES = 128
V_ROWS = 128
VMEM_LIMIT_BYTES = 56 * 1024 * 1024

TOK_TILE = 512
Q_TILE = 256
K_TILE = 512
DFT_TILE = 512

BF16 = jnp.bfloat16
F32 = jnp.float32


def _params(*semantics):
    return pltpu.CompilerParams(dimension_semantics=semantics,
                                vmem_limit_bytes=VMEM_LIMIT_BYTES)


def _rope_tables(seq_len):
    rows = seq_len // GRID_W
    row = jnp.repeat(jnp.arange(rows, dtype=F32), GRID_W)
    col = jnp.tile(jnp.arange(GRID_W, dtype=F32), rows)
    n_freq = HEAD_DIM // 4
    inv_freq = ROPE_THETA ** (-jnp.arange(n_freq, dtype=F32) / n_freq)
    ang_r = row[:, None] * inv_freq[None, :]
    ang_c = col[:, None] * inv_freq[None, :]
    ang = jnp.concatenate([ang_r, ang_r, ang_c, ang_c], axis=-1)
    sign = jnp.where((jnp.arange(HEAD_DIM) % 32) < 16, -1.0, 1.0).astype(F32)
    cos = jnp.cos(ang)
    sin = jnp.sin(ang) * sign[None, :]
    reps = LANES // HEAD_DIM
    return jnp.tile(cos, (1, reps)), jnp.tile(sin, (1, reps))


def _seq_dft_tables(seq_len):
    k = jnp.arange(seq_len, dtype=jnp.int32)[:, None]
    s1 = jnp.arange(seq_len // GRID_W, dtype=jnp.int32)[None, :]
    s2 = jnp.arange(GRID_W, dtype=jnp.int32)[None, :]
    step = 2.0 * np.pi / seq_len
    ang_hi = ((k * s1 * GRID_W) % seq_len).astype(F32) * step
    ang_lo = ((k * s2) % seq_len).astype(F32) * step
    ch, sh = jnp.cos(ang_hi)[:, :, None], jnp.sin(ang_hi)[:, :, None]
    cl, sl = jnp.cos(ang_lo)[:, None, :], jnp.sin(ang_lo)[:, None, :]
    ct = (ch * cl - sh * sl).reshape(seq_len, seq_len)
    st = -(sh * cl + ch * sl).reshape(seq_len, seq_len)
    return ct.astype(BF16), st.astype(BF16)


def _chan_dft_table(seq_len):
    idx = np.arange(FGROUP_DIM, dtype=np.int64)
    ang = ((idx[:, None] * idx[None, :]) % FGROUP_DIM).astype(np.float64) * (2.0 * np.pi / FGROUP_DIM)
    scale = 1.0 / np.sqrt(float(seq_len) * FGROUP_DIM)
    return (np.concatenate([np.cos(ang), np.sin(ang)], axis=0) * scale).astype(np.float32)


def _wz_kernel(cs_ref, w_ref, o_ref):
    for g in range(N_FGROUPS):
        o_ref[g] = jnp.dot(cs_ref[...], w_ref[g], precision=lax.Precision.HIGHEST,
                           preferred_element_type=F32)


def _rmsnorm(x, g):
    return x * lax.rsqrt(jnp.mean(x * x, axis=-1, keepdims=True) + EPS) * g


def _proj_in_kernel(x_ref, w_ref, gn_ref, gq_ref, gk_ref, cos_ref, sin_ref, bd_ref, wz_ref,
                    q_ref, k_ref, vt_ref, ga_ref, xcs_ref, gf_ref):
    u = _rmsnorm(x_ref[0], gn_ref[...]).astype(BF16)
    z = jnp.dot(u, w_ref[...], preferred_element_type=F32)
    o_k = ATTN_WIDTH
    o_v = o_k + KV_WIDTH
    o_ga = o_v + KV_WIDTH
    o_f = o_ga + ATTN_WIDTH
    o_gf = o_f + F_WIDTH

    cos = cos_ref[...]
    sin = sin_ref[...]
    lane = lax.broadcasted_iota(jnp.int32, (1, LANES), 1)
    first_half = (lane % 32) < 16

    def head_norm_rope(zz, g, width):
        msq = jnp.dot((zz * zz).astype(BF16), bd_ref[0:width, 0:width],
                      preferred_element_type=F32)
        n = zz * lax.rsqrt(msq + EPS) * g
        outs = []
        for c in range(width // LANES):
            xc = n[:, c * LANES:(c + 1) * LANES]
            rot = jnp.where(first_half, pltpu.roll(xc, LANES - 16, 1), pltpu.roll(xc, 16, 1))
            outs.append(xc * cos + rot * sin)
        return outs

    scale = HEAD_DIM ** -0.5
    for c, qc in enumerate(head_norm_rope(z[:, 0:o_k], gq_ref[...], ATTN_WIDTH)):
        qc = (qc * scale).astype(BF16)
        q_ref[0, 2 * c] = qc[:, 0:HEAD_DIM]
        q_ref[0, 2 * c + 1] = qc[:, HEAD_DIM:LANES]
    (kc,) = head_norm_rope(z[:, o_k:o_v], gk_ref[...], KV_WIDTH)
    kc = kc.astype(BF16)
    k_ref[0, 0] = kc[:, 0:HEAD_DIM]
    k_ref[0, 1] = kc[:, HEAD_DIM:LANES]

    vt = z[:, o_v:o_ga].T.astype(BF16)
    ones = jnp.ones((V_ROWS - HEAD_DIM, vt.shape[1]), BF16)
    for g in range(N_KV_HEADS):
        vt_ref[0, g, 0:HEAD_DIM, :] = vt[g * HEAD_DIM:(g + 1) * HEAD_DIM]
        vt_ref[0, g, HEAD_DIM:V_ROWS, :] = ones

    ga = z[:, o_ga:o_f]
    ga_ref[0] = (ga * jax.nn.sigmoid(ga)).astype(BF16)
    gf = z[:, o_gf:o_gf + F_WIDTH]
    gf_ref[0] = (gf * jax.nn.sigmoid(gf)).astype(BF16)
    xcs_ref[0] = jnp.dot(z[:, o_f:o_gf].astype(BF16), wz_ref[...],
                         preferred_element_type=F32).astype(BF16)


def _attn_kernel(q_ref, k_ref, vt_ref, o_ref, m_ref, acc_ref, *, k_tile):
    tq = q_ref.shape[2]
    seq = k_ref.shape[2]
    qall = q_ref[0].reshape(Q_PER_KV * tq, HEAD_DIM)
    m_ref[...] = jnp.full(m_ref.shape, -jnp.inf, F32)
    acc_ref[...] = jnp.zeros(acc_ref.shape, F32)

    def body(j, carry):
        off = pl.multiple_of(j * k_tile, k_tile)
        kj = k_ref[0, 0, pl.ds(off, k_tile), :]
        s = lax.dot_general(kj, qall, (((1,), (1,)), ((), ())),
                            preferred_element_type=F32)
        m_old = m_ref[...]
        m_new = jnp.maximum(m_old, jnp.max(s, axis=0, keepdims=True))
        alpha = jnp.exp(m_old - m_new)
        p = jnp.exp(s - m_new).astype(BF16)
        vj = vt_ref[0, 0, :, pl.ds(off, k_tile)]
        acc_ref[...] = alpha * acc_ref[...] + jnp.dot(vj, p, preferred_element_type=F32)
        m_ref[...] = m_new
        return carry

    lax.fori_loop(0, seq // k_tile, body, 0)

    acc = acc_ref[...]
    o = acc / acc[HEAD_DIM:HEAD_DIM + 1]
    for h in range(Q_PER_KV):
        oh = o[:, h * tq:(h + 1) * tq].T
        o_ref[0, :, h * HEAD_DIM:(h + 1) * HEAD_DIM] = oh[:, 0:HEAD_DIM].astype(BF16)


def _dft_kernel(ct_ref, st_ref, xcs_ref, gf_ref, fo_ref):
    y = jnp.dot(ct_ref[...], xcs_ref[0, :, 0:F_WIDTH], preferred_element_type=F32)
    y = y + jnp.dot(st_ref[...], xcs_ref[0, :, F_WIDTH:2 * F_WIDTH], preferred_element_type=F32)
    fo_ref[0] = (y * gf_ref[0].astype(F32)).astype(BF16)


def _proj_out_kernel(x_ref, a_ref, ga_ref, fo_ref, p_ref, wo_ref, wg_ref, wp_ref,
                     gple_ref, gfin_ref, y_ref, *, final_norm):
    mix_a = (a_ref[0].astype(F32) * ga_ref[0].astype(F32)).astype(BF16)
    h = x_ref[0] + jnp.dot(mix_a, wo_ref[0:ATTN_WIDTH, :], preferred_element_type=F32)
    h = h + jnp.dot(fo_ref[0], wo_ref[ATTN_WIDTH:ATTN_WIDTH + F_WIDTH, :],
                    preferred_element_type=F32)
    hn = _rmsnorm(h, gple_ref[...]).astype(BF16)
    gate = jax.nn.sigmoid(jnp.dot(hn, wg_ref[...], preferred_element_type=F32))
    ple = jnp.dot(p_ref[0].astype(BF16), wp_ref[...], preferred_element_type=F32)
    h = h + ple * gate
    if final_norm:
        h = _rmsnorm(h, gfin_ref[...])
    y_ref[0] = h


def _const_spec(shape):
    return pl.BlockSpec(shape, lambda *_: (0,) * len(shape))


def _fourier_weights(w_fmix, seq_len):
    cs = jnp.asarray(_chan_dft_table(seq_len))
    out = pl.pallas_call(
        _wz_kernel,
        out_shape=jax.ShapeDtypeStruct((N_FGROUPS, 2 * FGROUP_DIM, FGROUP_DIM), F32),
        name="fourier_weights",
    )(cs, w_fmix)
    eye = jnp.eye(N_FGROUPS, dtype=F32)

    def block_diag(w):
        return (eye[:, None, :, None] * w[:, :, None, :]).reshape(F_WIDTH, F_WIDTH)

    return jnp.concatenate([block_diag(out[:, :FGROUP_DIM]), block_diag(out[:, FGROUP_DIM:])],
                           axis=1).astype(BF16)


def _layer(x, p, w, *, final_norm):
    batch, seq, d_model = x.shape
    in_width = w["w_in"].shape[1]
    nt = seq // TOK_TILE
    tok = lambda width: pl.BlockSpec((1, TOK_TILE, width), lambda b, t: (b, t, 0))

    q, k, vt, ga, xcs, gf = pl.pallas_call(
        _proj_in_kernel,
        grid=(batch, nt),
        in_specs=[
            tok(d_model),
            _const_spec((d_model, in_width)),
            _const_spec((1, d_model)),
            _const_spec((1, ATTN_WIDTH)),
            _const_spec((1, KV_WIDTH)),
            pl.BlockSpec((TOK_TILE, LANES), lambda b, t: (t, 0)),
            pl.BlockSpec((TOK_TILE, LANES), lambda b, t: (t, 0)),
            _const_spec((ATTN_WIDTH, ATTN_WIDTH)),
            _const_spec((F_WIDTH, 2 * F_WIDTH)),
        ],
        out_specs=[
            pl.BlockSpec((1, N_HEADS, TOK_TILE, HEAD_DIM), lambda b, t: (b, 0, t, 0)),
            pl.BlockSpec((1, N_KV_HEADS, TOK_TILE, HEAD_DIM), lambda b, t: (b, 0, t, 0)),
            pl.BlockSpec((1, N_KV_HEADS, V_ROWS, TOK_TILE), lambda b, t: (b, 0, 0, t)),
            tok(ATTN_WIDTH),
            tok(2 * F_WIDTH),
            tok(F_WIDTH),
        ],
        out_shape=[
            jax.ShapeDtypeStruct((batch, N_HEADS, seq, HEAD_DIM), BF16),
            jax.ShapeDtypeStruct((batch, N_KV_HEADS, seq, HEAD_DIM), BF16),
            jax.ShapeDtypeStruct((batch, N_KV_HEADS, V_ROWS, seq), BF16),
            jax.ShapeDtypeStruct((batch, seq, ATTN_WIDTH), BF16),
            jax.ShapeDtypeStruct((batch, seq, 2 * F_WIDTH), BF16),
            jax.ShapeDtypeStruct((batch, seq, F_WIDTH), BF16),
        ],
        compiler_params=_params("parallel", "parallel"),
        name="proj_in",
    )(x, w["w_in"], w["g_norm"], w["g_q"], w["g_k"], w["cos"], w["sin"], w["bd"], w["wz"])

    kv_width = Q_PER_KV * HEAD_DIM
    a = pl.pallas_call(
        functools.partial(_attn_kernel, k_tile=K_TILE),
        grid=(batch, N_KV_HEADS, seq // Q_TILE),
        in_specs=[
            pl.BlockSpec((1, Q_PER_KV, Q_TILE, HEAD_DIM), lambda b, g, i: (b, g, i, 0)),
            pl.BlockSpec((1, 1, seq, HEAD_DIM), lambda b, g, i: (b, g, 0, 0)),
            pl.BlockSpec((1, 1, V_ROWS, seq), lambda b, g, i: (b, g, 0, 0)),
        ],
        out_specs=pl.BlockSpec((1, Q_TILE, kv_width), lambda b, g, i: (b, i, g)),
        out_shape=jax.ShapeDtypeStruct((batch, seq, ATTN_WIDTH), BF16),
        scratch_shapes=[pltpu.VMEM((1, Q_PER_KV * Q_TILE), F32),
                        pltpu.VMEM((V_ROWS, Q_PER_KV * Q_TILE), F32)],
        compiler_params=_params("parallel", "parallel", "parallel"),
        name="attn",
    )(q, k, vt)

    fo = pl.pallas_call(
        _dft_kernel,
        grid=(batch, seq // DFT_TILE),
        in_specs=[
            pl.BlockSpec((DFT_TILE, seq), lambda b, i: (i, 0)),
            pl.BlockSpec((DFT_TILE, seq), lambda b, i: (i, 0)),
            pl.BlockSpec((1, seq, 2 * F_WIDTH), lambda b, i: (b, 0, 0)),
            pl.BlockSpec((1, DFT_TILE, F_WIDTH), lambda b, i: (b, i, 0)),
        ],
        out_specs=pl.BlockSpec((1, DFT_TILE, F_WIDTH), lambda b, i: (b, i, 0)),
        out_shape=jax.ShapeDtypeStruct((batch, seq, F_WIDTH), BF16),
        compiler_params=_params("parallel", "parallel"),
        name="dft",
    )(w["ct"], w["st"], xcs, gf)

    ple_dim = p.shape[-1]
    return pl.pallas_call(
        functools.partial(_proj_out_kernel, final_norm=final_norm),
        grid=(batch, nt),
        in_specs=[
            tok(d_model), tok(ATTN_WIDTH), tok(ATTN_WIDTH), tok(F_WIDTH), tok(ple_dim),
            _const_spec((ATTN_WIDTH + F_WIDTH, d_model)),
            _const_spec((d_model, d_model)),
            _const_spec((ple_dim, d_model)),
            _const_spec((1, d_model)),
            _const_spec((1, d_model)),
        ],
        out_specs=tok(d_model),
        out_shape=jax.ShapeDtypeStruct((batch, seq, d_model), F32),
        compiler_params=_params("parallel", "parallel"),
        name="proj_out",
    )(x, a, ga, fo, p, w["w_out"], w["w_ple_gate"], w["w_ple"], w["g_ple"], w["g_final"])


def _layer_weights(i, seq, g_norm, w_in, g_q, g_k, w_fmix, w_out, g_ple, w_ple_gate, w_ple, g_final):
    cos, sin = _rope_tables(seq)
    ct, st = _seq_dft_tables(seq)
    heads = np.arange(ATTN_WIDTH) // HEAD_DIM
    bd = jnp.asarray((heads[:, None] == heads[None, :]).astype(np.float32) / HEAD_DIM, dtype=BF16)
    return {
        "w_in": w_in[i].astype(BF16),
        "g_norm": g_norm[i][None, :],
        "g_q": jnp.tile(g_q[i], N_HEADS)[None, :],
        "g_k": jnp.tile(g_k[i], N_KV_HEADS)[None, :],
        "cos": cos, "sin": sin, "bd": bd,
        "wz": _fourier_weights(w_fmix[i], seq),
        "ct": ct, "st": st,
        "w_out": w_out[i].astype(BF16),
        "w_ple_gate": w_ple_gate[i].astype(BF16),
        "w_ple": w_ple[i].astype(BF16),
        "g_ple": g_ple[i][None, :],
        "g_final": g_final[None, :],
    }


def _trunk(x, p, layer_weights):
    h = x
    depth = len(layer_weights)
    for i, w in enumerate(layer_weights):
        h = _layer(h, p[i], w, final_norm=(i == depth - 1))
    return h


def kernel(x_prompt, x_sample, p_prompt, p_sample, g_norm, w_in, g_q, g_k, w_fmix, w_out,
           g_ple, w_ple_gate, w_ple, g_final):
    depth = g_norm.shape[0]
    outs = []
    weights_by_seq = {}
    for x, p in ((x_prompt, p_prompt), (x_sample, p_sample)):
        seq = x.shape[1]
        if seq not in weights_by_seq:
            weights_by_seq[seq] = [
                _layer_weights(i, seq, g_norm, w_in, g_q, g_k, w_fmix, w_out, g_ple,
                               w_ple_gate, w_ple, g_final) for i in range(depth)]
        outs.append(_trunk(x, p, weights_by_seq[seq]))
    return tuple(outs)
```

```python
import functools

import jax
import jax.numpy as jnp
import numpy as np
from jax import lax
from jax.experimental import pallas as pl
from jax.experimental.pallas import tpu as pltpu

HEAD_DIM = 64
N_HEADS = 8
N_KV_HEADS = 2
Q_PER_KV = N_HEADS // N_KV_HEADS
ATTN_WIDTH = N_HEADS * HEAD_DIM
KV_WIDTH = N_KV_HEADS * HEAD_DIM
N_FGROUPS = 8
FGROUP_DIM = 64
F_WIDTH = N_FGROUPS * FGROUP_DIM
GRID_W = 64
ROPE_THETA = 10000.0
EPS = 1e-6

LANES = 128
V_ROWS = 80
VMEM_LIMIT_BYTES = 56 * 1024 * 1024

TOK_TILE = 512
Q_TILE = 1024
K_TILE = 512
Q_CHUNK = 256
DFT_TILE = 512

BF16 = jnp.bfloat16
F32 = jnp.float32


def _params(*semantics):
    return pltpu.CompilerParams(dimension_semantics=semantics,
                                vmem_limit_bytes=VMEM_LIMIT_BYTES)


def _rope_tables(seq_len):
    rows = seq_len // GRID_W
    row = jnp.repeat(jnp.arange(rows, dtype=F32), GRID_W)
    col = jnp.tile(jnp.arange(GRID_W, dtype=F32), rows)
    n_freq = HEAD_DIM // 4
    inv_freq = ROPE_THETA ** (-jnp.arange(n_freq, dtype=F32) / n_freq)
    ang_r = row[:, None] * inv_freq[None, :]
    ang_c = col[:, None] * inv_freq[None, :]
    ang = jnp.concatenate([ang_r, ang_r, ang_c, ang_c], axis=-1)
    sign = jnp.where((jnp.arange(HEAD_DIM) % 32) < 16, -1.0, 1.0).astype(F32)
    cos = jnp.cos(ang)
    sin = jnp.sin(ang) * sign[None, :]
    reps = LANES // HEAD_DIM
    return jnp.tile(cos, (1, reps)), jnp.tile(sin, (1, reps))


def _seq_dft_tables(seq_len):
    k = jnp.arange(seq_len, dtype=jnp.int32)[:, None]
    s1 = jnp.arange(seq_len // GRID_W, dtype=jnp.int32)[None, :]
    s2 = jnp.arange(GRID_W, dtype=jnp.int32)[None, :]
    step = 2.0 * np.pi / seq_len
    ang_hi = ((k * s1 * GRID_W) % seq_len).astype(F32) * step
    ang_lo = ((k * s2) % seq_len).astype(F32) * step
    ch, sh = jnp.cos(ang_hi)[:, :, None], jnp.sin(ang_hi)[:, :, None]
    cl, sl = jnp.cos(ang_lo)[:, None, :], jnp.sin(ang_lo)[:, None, :]
    ct = (ch * cl - sh * sl).reshape(seq_len, seq_len)
    st = -(sh * cl + ch * sl).reshape(seq_len, seq_len)
    return ct.astype(BF16), st.astype(BF16)


def _chan_dft_table(seq_len):
    idx = np.arange(FGROUP_DIM, dtype=np.int64)
    ang = ((idx[:, None] * idx[None, :]) % FGROUP_DIM).astype(np.float64) * (2.0 * np.pi / FGROUP_DIM)
    scale = 1.0 / np.sqrt(float(seq_len) * FGROUP_DIM)
    return (np.concatenate([np.cos(ang), np.sin(ang)], axis=0) * scale).astype(np.float32)


def _wz_kernel(cs_ref, w_ref, o_ref):
    for g in range(N_FGROUPS):
        o_ref[g] = jnp.dot(cs_ref[...], w_ref[g], precision=lax.Precision.HIGHEST,
                           preferred_element_type=F32)


def _rmsnorm(x, g):
    return x * lax.rsqrt(jnp.mean(x * x, axis=-1, keepdims=True) + EPS) * g


def _proj_in_kernel(x_ref, w_ref, gn_ref, gq_ref, gk_ref, cos_ref, sin_ref, bd_ref, wz_ref,
                    q_ref, k_ref, vt_ref, ga_ref, xcs_ref, gf_ref):
    u = _rmsnorm(x_ref[0], gn_ref[...]).astype(BF16)
    z = jnp.dot(u, w_ref[...], preferred_element_type=F32)
    o_k = ATTN_WIDTH
    o_v = o_k + KV_WIDTH
    o_ga = o_v + KV_WIDTH
    o_f = o_ga + ATTN_WIDTH
    o_gf = o_f + F_WIDTH

    cos = cos_ref[...]
    sin = sin_ref[...]
    lane = lax.broadcasted_iota(jnp.int32, (1, LANES), 1)
    first_half = (lane % 32) < 16

    def head_norm_rope(zz, g, width):
        msq = jnp.dot((zz * zz).astype(BF16), bd_ref[0:width, 0:width],
                      preferred_element_type=F32)
        n = zz * lax.rsqrt(msq + EPS) * g
        outs = []
        for c in range(width // LANES):
            xc = n[:, c * LANES:(c + 1) * LANES]
            rot = jnp.where(first_half, pltpu.roll(xc, LANES - 16, 1), pltpu.roll(xc, 16, 1))
            outs.append(xc * cos + rot * sin)
        return outs

    scale = HEAD_DIM ** -0.5 * np.log2(np.e)
    for c, qc in enumerate(head_norm_rope(z[:, 0:o_k], gq_ref[...], ATTN_WIDTH)):
        qt = (qc * scale).T.astype(BF16)
        q_ref[0, 2 * c] = qt[0:HEAD_DIM]
        q_ref[0, 2 * c + 1] = qt[HEAD_DIM:LANES]
    (kc,) = head_norm_rope(z[:, o_k:o_v], gk_ref[...], KV_WIDTH)
    kc = kc.astype(BF16)
    k_ref[0, 0] = kc[:, 0:HEAD_DIM]
    k_ref[0, 1] = kc[:, HEAD_DIM:LANES]

    vt = z[:, o_v:o_ga].T.astype(BF16)
    ones = jnp.ones((V_ROWS - HEAD_DIM, vt.shape[1]), BF16)
    for g in range(N_KV_HEADS):
        vt_ref[0, g, 0:HEAD_DIM, :] = vt[g * HEAD_DIM:(g + 1) * HEAD_DIM]
        vt_ref[0, g, HEAD_DIM:V_ROWS, :] = ones

    ga = z[:, o_ga:o_f]
    ga_ref[0] = (ga * jax.nn.sigmoid(ga)).astype(BF16)
    gf = z[:, o_gf:o_gf + F_WIDTH]
    gf_ref[0] = (gf * jax.nn.sigmoid(gf)).astype(BF16)
    xcs_ref[0] = jnp.dot(z[:, o_f:o_gf].astype(BF16), wz_ref[...],
                         preferred_element_type=F32).astype(BF16)


def _attn_kernel(q_ref, k_ref, vt_ref, o_ref, m_ref, acc_ref, s0_ref, s1_ref, *, k_tile, q_chunk):
    tq = q_ref.shape[3]
    seq = k_ref.shape[2]
    n_chunks = Q_PER_KV * tq // q_chunk
    n_tiles = seq // k_tile
    assert n_tiles % 2 == 0
    m_ref[...] = jnp.full(m_ref.shape, -jnp.inf, F32)
    acc_ref[...] = jnp.zeros(acc_ref.shape, F32)

    def scores(j, c, s_ref):
        h, r = divmod(c * q_chunk, tq)
        kj = k_ref[0, 0, pl.ds(pl.multiple_of(j * k_tile, k_tile), k_tile), :]
        s_ref[c] = jnp.dot(kj, q_ref[0, h, :, r:r + q_chunk], preferred_element_type=F32)

    def softmax_pv(j, c, s_ref):
        cols = slice(c * q_chunk, (c + 1) * q_chunk)
        vj = vt_ref[0, 0, :, pl.ds(pl.multiple_of(j * k_tile, k_tile), k_tile)]
        m_old = m_ref[:, cols]
        m_new = jnp.maximum(m_old, jnp.max(s_ref[c], axis=0, keepdims=True))
        alpha = jnp.exp2(m_old - m_new)
        p = jnp.exp2(s_ref[c] - m_new).astype(BF16)
        acc_ref[:, cols] = alpha * acc_ref[:, cols] + jnp.dot(vj, p, preferred_element_type=F32)
        m_ref[:, cols] = m_new

    def step(j, cur_ref, nxt_ref, prefetch=True):
        for c in range(n_chunks):
            if prefetch:
                scores(j + 1, c, nxt_ref)
            softmax_pv(j, c, cur_ref)

    for c in range(n_chunks):
        scores(0, c, s0_ref)

    def body(i, carry):
        step(2 * i, s0_ref, s1_ref)
        step(2 * i + 1, s1_ref, s0_ref)
        return carry

    lax.fori_loop(0, n_tiles // 2 - 1, body, 0)
    step(n_tiles - 2, s0_ref, s1_ref)
    step(n_tiles - 1, s1_ref, s0_ref, prefetch=False)

    pad = jnp.zeros((LANES - HEAD_DIM, tq), F32)
    for h in range(Q_PER_KV):
        acc = acc_ref[:, h * tq:(h + 1) * tq]
        o = acc[0:HEAD_DIM] / acc[HEAD_DIM:HEAD_DIM + 1]
        ot = jnp.concatenate([o, pad], axis=0).T
        o_ref[0, :, h * HEAD_DIM:(h + 1) * HEAD_DIM] = ot[:, 0:HEAD_DIM].astype(BF16)


def _dft_kernel(ct_ref, st_ref, xcs_ref, gf_ref, fo_ref):
    y = jnp.dot(ct_ref[...], xcs_ref[0, :, 0:F_WIDTH], preferred_element_type=F32)
    y = y + jnp.dot(st_ref[...], xcs_ref[0, :, F_WIDTH:2 * F_WIDTH], preferred_element_type=F32)
    fo_ref[0] = (y * gf_ref[0].astype(F32)).astype(BF16)


def _proj_out_kernel(x_ref, a_ref, ga_ref, fo_ref, p_ref, wo_ref, wg_ref, wp_ref,
                     gple_ref, gfin_ref, y_ref, *, final_norm):
    mix_a = (a_ref[0].astype(F32) * ga_ref[0].astype(F32)).astype(BF16)
    h = x_ref[0] + jnp.dot(mix_a, wo_ref[0:ATTN_WIDTH, :], preferred_element_type=F32)
    h = h + jnp.dot(fo_ref[0], wo_ref[ATTN_WIDTH:ATTN_WIDTH + F_WIDTH, :],
                    preferred_element_type=F32)
    hn = _rmsnorm(h, gple_ref[...]).astype(BF16)
    gate = jax.nn.sigmoid(jnp.dot(hn, wg_ref[...], preferred_element_type=F32))
    ple = jnp.dot(p_ref[0].astype(BF16), wp_ref[...], preferred_element_type=F32)
    h = h + ple * gate
    if final_norm:
        h = _rmsnorm(h, gfin_ref[...])
    y_ref[0] = h


def _const_spec(shape):
    return pl.BlockSpec(shape, lambda *_: (0,) * len(shape))


def _fourier_weights(w_fmix, seq_len):
    cs = jnp.asarray(_chan_dft_table(seq_len))
    out = pl.pallas_call(
        _wz_kernel,
        out_shape=jax.ShapeDtypeStruct((N_FGROUPS, 2 * FGROUP_DIM, FGROUP_DIM), F32),
        name="fourier_weights",
    )(cs, w_fmix)
    eye = jnp.eye(N_FGROUPS, dtype=F32)

    def block_diag(w):
        return (eye[:, None, :, None] * w[:, :, None, :]).reshape(F_WIDTH, F_WIDTH)

    return jnp.concatenate([block_diag(out[:, :FGROUP_DIM]), block_diag(out[:, FGROUP_DIM:])],
                           axis=1).astype(BF16)


def _layer(x, p, w, *, final_norm):
    batch, seq, d_model = x.shape
    in_width = w["w_in"].shape[1]
    nt = seq // TOK_TILE
    tok = lambda width: pl.BlockSpec((1, TOK_TILE, width), lambda b, t: (b, t, 0))

    q, k, vt, ga, xcs, gf = pl.pallas_call(
        _proj_in_kernel,
        grid=(batch, nt),
        in_specs=[
            tok(d_model),
            _const_spec((d_model, in_width)),
            _const_spec((1, d_model)),
            _const_spec((1, ATTN_WIDTH)),
            _const_spec((1, KV_WIDTH)),
            pl.BlockSpec((TOK_TILE, LANES), lambda b, t: (t, 0)),
            pl.BlockSpec((TOK_TILE, LANES), lambda b, t: (t, 0)),
            _const_spec((ATTN_WIDTH, ATTN_WIDTH)),
            _const_spec((F_WIDTH, 2 * F_WIDTH)),
        ],
        out_specs=[
            pl.BlockSpec((1, N_HEADS, HEAD_DIM, TOK_TILE), lambda b, t: (b, 0, 0, t)),
            pl.BlockSpec((1, N_KV_HEADS, TOK_TILE, HEAD_DIM), lambda b, t: (b, 0, t, 0)),
            pl.BlockSpec((1, N_KV_HEADS, V_ROWS, TOK_TILE), lambda b, t: (b, 0, 0, t)),
            tok(ATTN_WIDTH),
            tok(2 * F_WIDTH),
            tok(F_WIDTH),
        ],
        out_shape=[
            jax.ShapeDtypeStruct((batch, N_HEADS, HEAD_DIM, seq), BF16),
            jax.ShapeDtypeStruct((batch, N_KV_HEADS, seq, HEAD_DIM), BF16),
            jax.ShapeDtypeStruct((batch, N_KV_HEADS, V_ROWS, seq), BF16),
            jax.ShapeDtypeStruct((batch, seq, ATTN_WIDTH), BF16),
            jax.ShapeDtypeStruct((batch, seq, 2 * F_WIDTH), BF16),
            jax.ShapeDtypeStruct((batch, seq, F_WIDTH), BF16),
        ],
        compiler_params=_params("parallel", "parallel"),
        name="proj_in",
    )(x, w["w_in"], w["g_norm"], w["g_q"], w["g_k"], w["cos"], w["sin"], w["bd"], w["wz"])

    kv_width = Q_PER_KV * HEAD_DIM
    a = pl.pallas_call(
        functools.partial(_attn_kernel, k_tile=K_TILE, q_chunk=Q_CHUNK),
        grid=(batch, N_KV_HEADS, seq // Q_TILE),
        in_specs=[
            pl.BlockSpec((1, Q_PER_KV, HEAD_DIM, Q_TILE), lambda b, g, i: (b, g, 0, i)),
            pl.BlockSpec((1, 1, seq, HEAD_DIM), lambda b, g, i: (b, g, 0, 0)),
            pl.BlockSpec((1, 1, V_ROWS, seq), lambda b, g, i: (b, g, 0, 0)),
        ],
        out_specs=pl.BlockSpec((1, Q_TILE, kv_width), lambda b, g, i: (b, i, g)),
        out_shape=jax.ShapeDtypeStruct((batch, seq, ATTN_WIDTH), BF16),
        scratch_shapes=[pltpu.VMEM((1, Q_PER_KV * Q_TILE), F32),
                        pltpu.VMEM((V_ROWS, Q_PER_KV * Q_TILE), F32),
                        pltpu.VMEM((Q_PER_KV * Q_TILE // Q_CHUNK, K_TILE, Q_CHUNK), F32),
                        pltpu.VMEM((Q_PER_KV * Q_TILE // Q_CHUNK, K_TILE, Q_CHUNK), F32)],
        compiler_params=_params("parallel", "parallel", "parallel"),
        name="attn",
    )(q, k, vt)

    fo = pl.pallas_call(
        _dft_kernel,
        grid=(batch, seq // DFT_TILE),
        in_specs=[
            pl.BlockSpec((DFT_TILE, seq), lambda b, i: (i, 0)),
            pl.BlockSpec((DFT_TILE, seq), lambda b, i: (i, 0)),
            pl.BlockSpec((1, seq, 2 * F_WIDTH), lambda b, i: (b, 0, 0)),
            pl.BlockSpec((1, DFT_TILE, F_WIDTH), lambda b, i: (b, i, 0)),
        ],
        out_specs=pl.BlockSpec((1, DFT_TILE, F_WIDTH), lambda b, i: (b, i, 0)),
        out_shape=jax.ShapeDtypeStruct((batch, seq, F_WIDTH), BF16),
        compiler_params=_params("parallel", "parallel"),
        name="dft",
    )(w["ct"], w["st"], xcs, gf)

    ple_dim = p.shape[-1]
    return pl.pallas_call(
        functools.partial(_proj_out_kernel, final_norm=final_norm),
        grid=(batch, nt),
        in_specs=[
            tok(d_model), tok(ATTN_WIDTH), tok(ATTN_WIDTH), tok(F_WIDTH), tok(ple_dim),
            _const_spec((ATTN_WIDTH + F_WIDTH, d_model)),
            _const_spec((d_model, d_model)),
            _const_spec((ple_dim, d_model)),
            _const_spec((1, d_model)),
            _const_spec((1, d_model)),
        ],
        out_specs=tok(d_model),
        out_shape=jax.ShapeDtypeStruct((batch, seq, d_model), F32),
        compiler_params=_params("parallel", "parallel"),
        name="proj_out",
    )(x, a, ga, fo, p, w["w_out"], w["w_ple_gate"], w["w_ple"], w["g_ple"], w["g_final"])


def _layer_weights(i, seq, g_norm, w_in, g_q, g_k, w_fmix, w_out, g_ple, w_ple_gate, w_ple, g_final):
    cos, sin = _rope_tables(seq)
    ct, st = _seq_dft_tables(seq)
    heads = np.arange(ATTN_WIDTH) // HEAD_DIM
    bd = jnp.asarray((heads[:, None] == heads[None, :]).astype(np.float32) / HEAD_DIM, dtype=BF16)
    return {
        "w_in": w_in[i].astype(BF16),
        "g_norm": g_norm[i][None, :],
        "g_q": jnp.tile(g_q[i], N_HEADS)[None, :],
        "g_k": jnp.tile(g_k[i], N_KV_HEADS)[None, :],
        "cos": cos, "sin": sin, "bd": bd,
        "wz": _fourier_weights(w_fmix[i], seq),
        "ct": ct, "st": st,
        "w_out": w_out[i].astype(BF16),
        "w_ple_gate": w_ple_gate[i].astype(BF16),
        "w_ple": w_ple[i].astype(BF16),
        "g_ple": g_ple[i][None, :],
        "g_final": g_final[None, :],
    }


def _trunk(x, p, layer_weights):
    h = x
    depth = len(layer_weights)
    for i, w in enumerate(layer_weights):
        h = _layer(h, p[i], w, final_norm=(i == depth - 1))
    return h


def kernel(x_prompt, x_sample, p_prompt, p_sample, g_norm, w_in, g_q, g_k, w_fmix, w_out,
           g_ple, w_ple_gate, w_ple, g_final):
    depth = g_norm.shape[0]
    outs = []
    weights_by_seq = {}
    for x, p in ((x_prompt, p_prompt), (x_sample, p_sample)):
        seq = x.shape[1]
        if seq not in weights_by_seq:
            weights_by_seq[seq] = [
                _layer_weights(i, seq, g_norm, w_in, g_q, g_k, w_fmix, w_out, g_ple,
                               w_ple_gate, w_ple, g_final) for i in range(depth)]
        outs.append(_trunk(x, p, weights_by_seq[seq]))
    return tuple(outs)
```

```python
import functools

import jax
import jax.numpy as jnp
import numpy as np
from jax import lax
from jax.experimental import pallas as pl
from jax.experimental.pallas import tpu as pltpu

HEAD_DIM = 64
N_HEADS = 8
N_KV_HEADS = 2
Q_PER_KV = N_HEADS // N_KV_HEADS
ATTN_WIDTH = N_HEADS * HEAD_DIM
KV_WIDTH = N_KV_HEADS * HEAD_DIM
N_FGROUPS = 8
FGROUP_DIM = 64
F_WIDTH = N_FGROUPS * FGROUP_DIM
GRID_W = 64
ROPE_THETA = 10000.0
EPS = 1e-6

LANES = 128
MXU_DIM = 256
ROW_SPLIT = 2
V_ROWS = 80
VMEM_LIMIT_BYTES = 56 * 1024 * 1024

TOK_TILE = 512
Q_TILE = 1024
K_TILE = 512
Q_CHUNK = 256
DFT_TILE = 512
REV_BLOCK = 256

BF16 = jnp.bfloat16
F32 = jnp.float32


def _params(*semantics):
    return pltpu.CompilerParams(dimension_semantics=semantics,
                                vmem_limit_bytes=VMEM_LIMIT_BYTES)


def _rope_tables(seq_len):
    rows = seq_len // GRID_W
    row = jnp.repeat(jnp.arange(rows, dtype=F32), GRID_W)
    col = jnp.tile(jnp.arange(GRID_W, dtype=F32), rows)
    n_freq = HEAD_DIM // 4
    inv_freq = ROPE_THETA ** (-jnp.arange(n_freq, dtype=F32) / n_freq)
    ang_r = row[:, None] * inv_freq[None, :]
    ang_c = col[:, None] * inv_freq[None, :]
    ang = jnp.concatenate([ang_r, ang_r, ang_c, ang_c], axis=-1)
    sign = jnp.where((jnp.arange(HEAD_DIM) % 32) < 16, -1.0, 1.0).astype(F32)
    cos = jnp.cos(ang)
    sin = jnp.sin(ang) * sign[None, :]
    reps = LANES // HEAD_DIM
    return jnp.tile(cos, (1, reps)), jnp.tile(sin, (1, reps))


def _seq_dft_tables(seq_len):
    half = seq_len // 2
    k = jnp.arange(half, dtype=jnp.int32)[:, None]
    s1 = jnp.arange(half // GRID_W, dtype=jnp.int32)[None, :]
    s2 = jnp.arange(GRID_W, dtype=jnp.int32)[None, :]
    step = 2.0 * np.pi / seq_len
    ang_hi = ((k * s1 * GRID_W) % seq_len).astype(F32) * step
    ang_lo = ((k * s2) % seq_len).astype(F32) * step
    ch, sh = jnp.cos(ang_hi)[:, :, None], jnp.sin(ang_hi)[:, :, None]
    cl, sl = jnp.cos(ang_lo)[:, None, :], jnp.sin(ang_lo)[:, None, :]
    col_weight = np.ones((1, half), np.float32)
    col_weight[0, 0] = 0.5
    ct = (ch * cl - sh * sl).reshape(half, half) * col_weight
    st = (sh * cl + ch * sl).reshape(half, half)
    alt = np.zeros((16, half), np.float32)
    alt[0] = (1.0 - 2.0 * (np.arange(half) % 2)) * col_weight[0]
    return ct.astype(BF16), st.astype(BF16), jnp.asarray(alt, dtype=BF16)


def _shifted_reversal(n):
    r = np.zeros((n, n), np.float32)
    t = np.arange(1, n)
    r[t, n - t] = 1.0
    return jnp.asarray(r, dtype=BF16)


def _chan_dft_table(seq_len):
    idx = np.arange(FGROUP_DIM, dtype=np.int64)
    ang = ((idx[:, None] * idx[None, :]) % FGROUP_DIM).astype(np.float64) * (2.0 * np.pi / FGROUP_DIM)
    scale = 1.0 / np.sqrt(float(seq_len) * FGROUP_DIM)
    return (np.concatenate([np.cos(ang), np.sin(ang)], axis=0) * scale).astype(np.float32)


def _wz_kernel(cs_ref, w_ref, o_ref):
    for g in range(N_FGROUPS):
        o_ref[g] = jnp.dot(cs_ref[...], w_ref[g], precision=lax.Precision.HIGHEST,
                           preferred_element_type=F32)


def _rmsnorm(x, g):
    return x * lax.rsqrt(jnp.mean(x * x, axis=-1, keepdims=True) + EPS) * g


def _proj_in_kernel(x_ref, w_ref, gn_ref, gq_ref, gk_ref, cos_ref, sin_ref, bd_ref, wz_ref,
                    q_ref, k_ref, vt_ref, ga_ref, xcs_ref, gf_ref):
    o_k = ATTN_WIDTH
    o_v = o_k + KV_WIDTH
    o_ga = o_v + KV_WIDTH
    o_f = o_ga + ATTN_WIDTH
    o_gf = o_f + F_WIDTH
    lane = lax.broadcasted_iota(jnp.int32, (1, LANES), 1)
    first_half = (lane % 32) < 16
    scale = HEAD_DIM ** -0.5 * np.log2(np.e)
    cos = cos_ref[...]
    sin = sin_ref[...]

    u = _rmsnorm(x_ref[0], gn_ref[...]).astype(BF16)

    def project(lo, hi):
        return jnp.dot(u, w_ref[:, lo:hi], preferred_element_type=F32)

    def head_mean_sq(zz, width):
        sq = (zz * zz).astype(BF16)
        step = min(width, MXU_DIM)
        return jnp.concatenate(
            [jnp.dot(sq[:, t:t + step], bd_ref[t:t + step, t:t + step],
                     preferred_element_type=F32) for t in range(0, width, step)], axis=1)

    def norm_rope(zz, msq, g):
        n = zz * lax.rsqrt(msq + EPS) * g
        outs = []
        for c in range(zz.shape[1] // LANES):
            xc = n[:, c * LANES:(c + 1) * LANES]
            rot = jnp.where(first_half, pltpu.roll(xc, LANES - 16, 1), pltpu.roll(xc, 16, 1))
            outs.append(xc * cos + rot * sin)
        return outs

    z_qkv = project(0, o_ga)
    z_ga = project(o_ga, o_f)
    msq_q = head_mean_sq(z_qkv[:, 0:o_k], ATTN_WIDTH)
    msq_k = head_mean_sq(z_qkv[:, o_k:o_v], KV_WIDTH)
    z_f = project(o_f, o_gf)

    vt = z_qkv[:, o_v:o_ga].T.astype(BF16)
    ones = jnp.ones((V_ROWS - HEAD_DIM, vt.shape[1]), BF16)
    for g in range(N_KV_HEADS):
        vt_ref[0, g, 0:HEAD_DIM, :] = vt[g * HEAD_DIM:(g + 1) * HEAD_DIM]
        vt_ref[0, g, HEAD_DIM:V_ROWS, :] = ones
    ga_ref[0] = (z_ga * jax.nn.sigmoid(z_ga)).astype(BF16)

    z_gf = project(o_gf, o_gf + F_WIDTH)
    f = z_f.astype(BF16)
    for t in range(0, F_WIDTH, MXU_DIM):
        for half in (0, F_WIDTH):
            xcs_ref[0, :, half + t:half + t + MXU_DIM] = jnp.dot(
                f[:, t:t + MXU_DIM], wz_ref[t:t + MXU_DIM, half + t:half + t + MXU_DIM],
                preferred_element_type=F32).astype(BF16)

    for c, qc in enumerate(norm_rope(z_qkv[:, 0:o_k], msq_q, gq_ref[...])):
        qt = (qc * scale).T.astype(BF16)
        q_ref[0, 2 * c] = qt[0:HEAD_DIM]
        q_ref[0, 2 * c + 1] = qt[HEAD_DIM:LANES]
    (kc,) = norm_rope(z_qkv[:, o_k:o_v], msq_k, gk_ref[...])
    kc = kc.astype(BF16)
    k_ref[0, 0] = kc[:, 0:HEAD_DIM]
    k_ref[0, 1] = kc[:, HEAD_DIM:LANES]
    gf_ref[0] = (z_gf * jax.nn.sigmoid(z_gf)).astype(BF16)


def _attn_kernel(q_ref, k_ref, vt_ref, o_ref, m_ref, acc_ref, s0_ref, s1_ref, *, k_tile, q_chunk):
    tq = q_ref.shape[3]
    seq = k_ref.shape[2]
    n_chunks = Q_PER_KV * tq // q_chunk
    n_tiles = seq // k_tile
    assert n_tiles % 2 == 0
    m_ref[...] = jnp.full(m_ref.shape, -jnp.inf, F32)
    acc_ref[...] = jnp.zeros(acc_ref.shape, F32)

    def scores(j, c, s_ref):
        h, r = divmod(c * q_chunk, tq)
        kj = k_ref[0, 0, pl.ds(pl.multiple_of(j * k_tile, k_tile), k_tile), :]
        s_ref[c] = jnp.dot(kj, q_ref[0, h, :, r:r + q_chunk], preferred_element_type=F32)

    def softmax_pv(j, c, s_ref):
        cols = slice(c * q_chunk, (c + 1) * q_chunk)
        vj = vt_ref[0, 0, :, pl.ds(pl.multiple_of(j * k_tile, k_tile), k_tile)]
        m_old = m_ref[:, cols]
        m_new = jnp.maximum(m_old, jnp.max(s_ref[c], axis=0, keepdims=True))
        alpha = jnp.exp2(m_old - m_new)
        p = jnp.exp2(s_ref[c] - m_new).astype(BF16)
        acc_ref[:, cols] = alpha * acc_ref[:, cols] + jnp.dot(vj, p, preferred_element_type=F32)
        m_ref[:, cols] = m_new

    def step(j, cur_ref, nxt_ref, prefetch=True):
        for c in range(n_chunks):
            if prefetch:
                scores(j + 1, c, nxt_ref)
            softmax_pv(j, c, cur_ref)

    for c in range(n_chunks):
        scores(0, c, s0_ref)

    def body(i, carry):
        step(2 * i, s0_ref, s1_ref)
        step(2 * i + 1, s1_ref, s0_ref)
        return carry

    lax.fori_loop(0, n_tiles // 2 - 1, body, 0)
    step(n_tiles - 2, s0_ref, s1_ref)
    step(n_tiles - 1, s1_ref, s0_ref, prefetch=False)

    pad = jnp.zeros((LANES - HEAD_DIM, tq), F32)
    for h in range(Q_PER_KV):
        acc = acc_ref[:, h * tq:(h + 1) * tq]
        o = acc[0:HEAD_DIM] / acc[HEAD_DIM:HEAD_DIM + 1]
        ot = jnp.concatenate([o, pad], axis=0).T
        o_ref[0, :, h * HEAD_DIM:(h + 1) * HEAD_DIM] = ot[:, 0:HEAD_DIM].astype(BF16)


def _dft_kernel(ct_ref, st_ref, alt_ref, rev_ref, x_ref, y_ref, pq_ref, u_ref):
    seq = x_ref.shape[1]
    half = seq // 2
    nb = half // REV_BLOCK
    rev = rev_ref[...]

    def is_row0(width):
        return lax.broadcasted_iota(jnp.int32, (REV_BLOCK, width), 0) == 0

    for i in range(nb):
        lo = slice(i * REV_BLOCK, (i + 1) * REV_BLOCK)
        mirror = jnp.dot(rev, x_ref[0, (2 * nb - 1 - i) * REV_BLOCK:(2 * nb - i) * REV_BLOCK, :],
                         preferred_element_type=F32)
        first = ((2 * nb - i) * REV_BLOCK) % seq
        mirror = jnp.where(is_row0(2 * F_WIDTH), x_ref[0, first:first + 1, :].astype(F32), mirror)
        xb = x_ref[0, lo, :].astype(F32)
        pq_ref[lo, 0:F_WIDTH] = (xb[:, 0:F_WIDTH] + mirror[:, 0:F_WIDTH]).astype(BF16)
        pq_ref[lo, F_WIDTH:] = (xb[:, F_WIDTH:] - mirror[:, F_WIDTH:]).astype(BF16)

    p_mid = x_ref[0, half:half + 1, 0:F_WIDTH].astype(F32)
    odd_k = (lax.broadcasted_iota(jnp.int32, (DFT_TILE, F_WIDTH), 0) & 1) == 1
    p_mid_signed = jnp.where(odd_k, -p_mid, p_mid)
    for kb in range(half // DFT_TILE):
        rows = slice(kb * DFT_TILE, (kb + 1) * DFT_TILE)
        a = jnp.dot(ct_ref[rows, :], pq_ref[:, 0:F_WIDTH], preferred_element_type=F32)
        a = a + p_mid_signed
        b = jnp.dot(st_ref[rows, :], pq_ref[:, F_WIDTH:], preferred_element_type=F32)
        y_ref[0, rows, :] = (a - b).astype(BF16)
        u_ref[rows, :] = (a + b).astype(BF16)

    y_mid = jnp.dot(alt_ref[...], pq_ref[:, 0:F_WIDTH], preferred_element_type=F32)[0:1] + p_mid

    for i in range(nb):
        mirror = jnp.dot(rev, u_ref[(nb - 1 - i) * REV_BLOCK:(nb - i) * REV_BLOCK, :],
                         preferred_element_type=F32)
        if i == 0:
            first_row = y_mid
        else:
            first_row = u_ref[(nb - i) * REV_BLOCK:(nb - i) * REV_BLOCK + 1, :].astype(F32)
        y_ref[0, half + i * REV_BLOCK:half + (i + 1) * REV_BLOCK, :] = jnp.where(
            is_row0(F_WIDTH), first_row, mirror).astype(BF16)


def _proj_out_kernel(x_ref, a_ref, ga_ref, fm_ref, gf_ref, p_ref, wo_ref, wg_ref, wp_ref,
                     gple_ref, gfin_ref, y_ref, *, final_norm):
    sub = x_ref.shape[1] // ROW_SPLIT
    halves = [slice(r * sub, (r + 1) * sub) for r in range(ROW_SPLIT)]
    hs, ples = [], []
    for rows in halves:
        mix_a = (a_ref[0, rows, :].astype(F32) * ga_ref[0, rows, :].astype(F32)).astype(BF16)
        mix_f = (fm_ref[0, rows, :].astype(F32) * gf_ref[0, rows, :].astype(F32)).astype(BF16)
        h = x_ref[0, rows, :] + jnp.dot(mix_a, wo_ref[0:ATTN_WIDTH, :],
                                        preferred_element_type=F32)
        hs.append(h + jnp.dot(mix_f, wo_ref[ATTN_WIDTH:ATTN_WIDTH + F_WIDTH, :],
                              preferred_element_type=F32))
        ples.append(jnp.dot(p_ref[0, rows, :].astype(BF16), wp_ref[...],
                            preferred_element_type=F32))
    for rows, h, ple in zip(halves, hs, ples):
        hn = _rmsnorm(h, gple_ref[...]).astype(BF16)
        gate = jax.nn.sigmoid(jnp.dot(hn, wg_ref[...], preferred_element_type=F32))
        h = h + ple * gate
        if final_norm:
            h = _rmsnorm(h, gfin_ref[...])
        y_ref[0, rows, :] = h


def _const_spec(shape):
    return pl.BlockSpec(shape, lambda *_: (0,) * len(shape))


def _fourier_weights(w_fmix, seq_len):
    cs = jnp.asarray(_chan_dft_table(seq_len))
    out = pl.pallas_call(
        _wz_kernel,
        out_shape=jax.ShapeDtypeStruct((N_FGROUPS, 2 * FGROUP_DIM, FGROUP_DIM), F32),
        name="fourier_weights",
    )(cs, w_fmix)
    eye = jnp.eye(N_FGROUPS, dtype=F32)

    def block_diag(w):
        return (eye[:, None, :, None] * w[:, :, None, :]).reshape(F_WIDTH, F_WIDTH)

    return jnp.concatenate([block_diag(out[:, :FGROUP_DIM]), block_diag(out[:, FGROUP_DIM:])],
                           axis=1).astype(BF16)


def _layer(x, p, w, *, final_norm):
    batch, seq, d_model = x.shape
    in_width = w["w_in"].shape[1]
    nt = seq // TOK_TILE
    tok = lambda width: pl.BlockSpec((1, TOK_TILE, width), lambda b, t: (b, t, 0))

    q, k, vt, ga, xcs, gf = pl.pallas_call(
        _proj_in_kernel,
        grid=(batch, nt),
        in_specs=[
            tok(d_model),
            _const_spec((d_model, in_width)),
            _const_spec((1, d_model)),
            _const_spec((1, ATTN_WIDTH)),
            _const_spec((1, KV_WIDTH)),
            pl.BlockSpec((TOK_TILE, LANES), lambda b, t: (t, 0)),
            pl.BlockSpec((TOK_TILE, LANES), lambda b, t: (t, 0)),
            _const_spec((ATTN_WIDTH, ATTN_WIDTH)),
            _const_spec((F_WIDTH, 2 * F_WIDTH)),
        ],
        out_specs=[
            pl.BlockSpec((1, N_HEADS, HEAD_DIM, TOK_TILE), lambda b, t: (b, 0, 0, t)),
            pl.BlockSpec((1, N_KV_HEADS, TOK_TILE, HEAD_DIM), lambda b, t: (b, 0, t, 0)),
            pl.BlockSpec((1, N_KV_HEADS, V_ROWS, TOK_TILE), lambda b, t: (b, 0, 0, t)),
            tok(ATTN_WIDTH),
            tok(2 * F_WIDTH),
            tok(F_WIDTH),
        ],
        out_shape=[
            jax.ShapeDtypeStruct((batch, N_HEADS, HEAD_DIM, seq), BF16),
            jax.ShapeDtypeStruct((batch, N_KV_HEADS, seq, HEAD_DIM), BF16),
            jax.ShapeDtypeStruct((batch, N_KV_HEADS, V_ROWS, seq), BF16),
            jax.ShapeDtypeStruct((batch, seq, ATTN_WIDTH), BF16),
            jax.ShapeDtypeStruct((batch, seq, 2 * F_WIDTH), BF16),
            jax.ShapeDtypeStruct((batch, seq, F_WIDTH), BF16),
        ],
        compiler_params=_params("parallel", "parallel"),
        name="proj_in",
    )(x, w["w_in"], w["g_norm"], w["g_q"], w["g_k"], w["cos"], w["sin"], w["bd"], w["wz"])

    kv_width = Q_PER_KV * HEAD_DIM
    a = pl.pallas_call(
        functools.partial(_attn_kernel, k_tile=K_TILE, q_chunk=Q_CHUNK),
        grid=(batch, N_KV_HEADS, seq // Q_TILE),
        in_specs=[
            pl.BlockSpec((1, Q_PER_KV, HEAD_DIM, Q_TILE), lambda b, g, i: (b, g, 0, i)),
            pl.BlockSpec((1, 1, seq, HEAD_DIM), lambda b, g, i: (b, g, 0, 0)),
            pl.BlockSpec((1, 1, V_ROWS, seq), lambda b, g, i: (b, g, 0, 0)),
        ],
        out_specs=pl.BlockSpec((1, Q_TILE, kv_width), lambda b, g, i: (b, i, g)),
        out_shape=jax.ShapeDtypeStruct((batch, seq, ATTN_WIDTH), BF16),
        scratch_shapes=[pltpu.VMEM((1, Q_PER_KV * Q_TILE), F32),
                        pltpu.VMEM((V_ROWS, Q_PER_KV * Q_TILE), F32),
                        pltpu.VMEM((Q_PER_KV * Q_TILE // Q_CHUNK, K_TILE, Q_CHUNK), F32),
                        pltpu.VMEM((Q_PER_KV * Q_TILE // Q_CHUNK, K_TILE, Q_CHUNK), F32)],
        compiler_params=_params("parallel", "parallel", "parallel"),
        name="attn",
    )(q, k, vt)

    half = seq // 2
    resident = lambda shape: pl.BlockSpec(shape, lambda b: (0,) * len(shape),
                                          pipeline_mode=pl.Buffered(1))
    fm = pl.pallas_call(
        _dft_kernel,
        grid=(batch,),
        in_specs=[
            resident((half, half)),
            resident((half, half)),
            resident(w["alt"].shape),
            resident((REV_BLOCK, REV_BLOCK)),
            pl.BlockSpec((1, seq, 2 * F_WIDTH), lambda b: (b, 0, 0)),
        ],
        out_specs=pl.BlockSpec((1, seq, F_WIDTH), lambda b: (b, 0, 0)),
        out_shape=jax.ShapeDtypeStruct((batch, seq, F_WIDTH), BF16),
        scratch_shapes=[pltpu.VMEM((half, 2 * F_WIDTH), BF16),
                        pltpu.VMEM((half, F_WIDTH), BF16)],
        compiler_params=_params("parallel"),
        name="dft",
    )(w["ct"], w["st"], w["alt"], w["rev"], xcs)

    ple_dim = p.shape[-1]
    return pl.pallas_call(
        functools.partial(_proj_out_kernel, final_norm=final_norm),
        grid=(batch, nt),
        in_specs=[
            tok(d_model), tok(ATTN_WIDTH), tok(ATTN_WIDTH), tok(F_WIDTH), tok(F_WIDTH),
            tok(ple_dim),
            _const_spec((ATTN_WIDTH + F_WIDTH, d_model)),
            _const_spec((d_model, d_model)),
            _const_spec((ple_dim, d_model)),
            _const_spec((1, d_model)),
            _const_spec((1, d_model)),
        ],
        out_specs=tok(d_model),
        out_shape=jax.ShapeDtypeStruct((batch, seq, d_model), F32),
        compiler_params=_params("parallel", "parallel"),
        name="proj_out",
    )(x, a, ga, fm, gf, p, w["w_out"], w["w_ple_gate"], w["w_ple"], w["g_ple"], w["g_final"])


def _layer_weights(i, seq, g_norm, w_in, g_q, g_k, w_fmix, w_out, g_ple, w_ple_gate, w_ple, g_final):
    cos, sin = _rope_tables(seq)
    ct, st, alt = _seq_dft_tables(seq)
    heads = np.arange(ATTN_WIDTH) // HEAD_DIM
    bd = jnp.asarray((heads[:, None] == heads[None, :]).astype(np.float32) / HEAD_DIM, dtype=BF16)
    return {
        "w_in": w_in[i].astype(BF16),
        "g_norm": g_norm[i][None, :],
        "g_q": jnp.tile(g_q[i], N_HEADS)[None, :],
        "g_k": jnp.tile(g_k[i], N_KV_HEADS)[None, :],
        "cos": cos, "sin": sin, "bd": bd,
        "wz": _fourier_weights(w_fmix[i], seq),
        "ct": ct, "st": st, "alt": alt, "rev": _shifted_reversal(REV_BLOCK),
        "w_out": w_out[i].astype(BF16),
        "w_ple_gate": w_ple_gate[i].astype(BF16),
        "w_ple": w_ple[i].astype(BF16),
        "g_ple": g_ple[i][None, :],
        "g_final": g_final[None, :],
    }


def _trunk(x, p, layer_weights):
    h = x
    depth = len(layer_weights)
    for i, w in enumerate(layer_weights):
        h = _layer(h, p[i], w, final_norm=(i == depth - 1))
    return h


def kernel(x_prompt, x_sample, p_prompt, p_sample, g_norm, w_in, g_q, g_k, w_fmix, w_out,
           g_ple, w_ple_gate, w_ple, g_final):
    depth = g_norm.shape[0]
    outs = []
    weights_by_seq = {}
    for x, p in ((x_prompt, p_prompt), (x_sample, p_sample)):
        seq = x.shape[1]
        if seq not in weights_by_seq:
            weights_by_seq[seq] = [
                _layer_weights(i, seq, g_norm, w_in, g_q, g_k, w_fmix, w_out, g_ple,
                               w_ple_gate, w_ple, g_final) for i in range(depth)]
        outs.append(_trunk(x, p, weights_by_seq[seq]))
    return tuple(outs)
```

```python
import functools

import jax
import jax.numpy as jnp
import numpy as np
from jax import lax
from jax.experimental import pallas as pl
from jax.experimental.pallas import tpu as pltpu

HEAD_DIM = 64
N_HEADS = 8
N_KV_HEADS = 2
Q_PER_KV = N_HEADS // N_KV_HEADS
ATTN_WIDTH = N_HEADS * HEAD_DIM
KV_WIDTH = N_KV_HEADS * HEAD_DIM
N_FGROUPS = 8
FGROUP_DIM = 64
F_WIDTH = N_FGROUPS * FGROUP_DIM
GRID_W = 64
ROPE_THETA = 10000.0
EPS = 1e-6

LANES = 128
MXU_DIM = 256
ROW_SPLIT = 2
V_ROWS = 80
VMEM_LIMIT_BYTES = 56 * 1024 * 1024

TOK_TILE = 1024
Q_TILE = 1024
K_TILE = 512
Q_CHUNK = 256
DFT_TILE = 512
REV_BLOCK = 256

BF16 = jnp.bfloat16
F32 = jnp.float32


def _params(*semantics):
    return pltpu.CompilerParams(dimension_semantics=semantics,
                                vmem_limit_bytes=VMEM_LIMIT_BYTES)


def _rope_tables(seq_len):
    rows = seq_len // GRID_W
    row = jnp.repeat(jnp.arange(rows, dtype=F32), GRID_W)
    col = jnp.tile(jnp.arange(GRID_W, dtype=F32), rows)
    n_freq = HEAD_DIM // 4
    inv_freq = ROPE_THETA ** (-jnp.arange(n_freq, dtype=F32) / n_freq)
    ang_r = row[:, None] * inv_freq[None, :]
    ang_c = col[:, None] * inv_freq[None, :]
    ang = jnp.concatenate([ang_r, ang_r, ang_c, ang_c], axis=-1)
    sign = jnp.where((jnp.arange(HEAD_DIM) % 32) < 16, -1.0, 1.0).astype(F32)
    cos = jnp.cos(ang)
    sin = jnp.sin(ang) * sign[None, :]
    reps = LANES // HEAD_DIM
    return jnp.tile(cos, (1, reps)), jnp.tile(sin, (1, reps))


def _seq_dft_tables(seq_len):
    half = seq_len // 2
    k = jnp.arange(half, dtype=jnp.int32)[:, None]
    s1 = jnp.arange(half // GRID_W, dtype=jnp.int32)[None, :]
    s2 = jnp.arange(GRID_W, dtype=jnp.int32)[None, :]
    step = 2.0 * np.pi / seq_len
    ang_hi = ((k * s1 * GRID_W) % seq_len).astype(F32) * step
    ang_lo = ((k * s2) % seq_len).astype(F32) * step
    ch, sh = jnp.cos(ang_hi)[:, :, None], jnp.sin(ang_hi)[:, :, None]
    cl, sl = jnp.cos(ang_lo)[:, None, :], jnp.sin(ang_lo)[:, None, :]
    col_weight = np.ones((1, half), np.float32)
    col_weight[0, 0] = 0.5
    ct = (ch * cl - sh * sl).reshape(half, half) * col_weight
    st = (sh * cl + ch * sl).reshape(half, half)
    alt = np.zeros((16, half), np.float32)
    alt[0] = (1.0 - 2.0 * (np.arange(half) % 2)) * col_weight[0]
    return ct.astype(BF16), st.astype(BF16), jnp.asarray(alt, dtype=BF16)


def _shifted_reversal(n):
    r = np.zeros((n, n), np.float32)
    t = np.arange(1, n)
    r[t, n - t] = 1.0
    return jnp.asarray(r, dtype=BF16)


def _chan_dft_table(seq_len):
    idx = np.arange(FGROUP_DIM, dtype=np.int64)
    ang = ((idx[:, None] * idx[None, :]) % FGROUP_DIM).astype(np.float64) * (2.0 * np.pi / FGROUP_DIM)
    scale = 1.0 / np.sqrt(float(seq_len) * FGROUP_DIM)
    return (np.concatenate([np.cos(ang), np.sin(ang)], axis=0) * scale).astype(np.float32)


def _wz_kernel(cs_ref, w_ref, o_ref):
    for g in range(N_FGROUPS):
        o_ref[g] = jnp.dot(cs_ref[...], w_ref[g], precision=lax.Precision.HIGHEST,
                           preferred_element_type=F32)


def _rmsnorm(x, g):
    return x * lax.rsqrt(jnp.mean(x * x, axis=-1, keepdims=True) + EPS) * g


def _proj_in_kernel(x_ref, w_ref, gn_ref, gq_ref, gk_ref, cos_ref, sin_ref, bd_ref, wz_ref,
                    q_ref, k_ref, vt_ref, ga_ref, xcs_ref, gf_ref):
    o_k = ATTN_WIDTH
    o_v = o_k + KV_WIDTH
    o_ga = o_v + KV_WIDTH
    o_f = o_ga + ATTN_WIDTH
    o_gf = o_f + F_WIDTH
    lane = lax.broadcasted_iota(jnp.int32, (1, LANES), 1)
    first_half = (lane % 32) < 16
    scale = HEAD_DIM ** -0.5 * np.log2(np.e)
    cos = cos_ref[...]
    sin = sin_ref[...]

    u = _rmsnorm(x_ref[0], gn_ref[...]).astype(BF16)

    def project(lo, hi):
        return jnp.dot(u, w_ref[:, lo:hi], preferred_element_type=F32)

    def head_mean_sq(zz, width):
        sq = (zz * zz).astype(BF16)
        step = min(width, MXU_DIM)
        return jnp.concatenate(
            [jnp.dot(sq[:, t:t + step], bd_ref[t:t + step, t:t + step],
                     preferred_element_type=F32) for t in range(0, width, step)], axis=1)

    def norm_rope(zz, msq, g):
        n = zz * lax.rsqrt(msq + EPS) * g
        outs = []
        for c in range(zz.shape[1] // LANES):
            xc = n[:, c * LANES:(c + 1) * LANES]
            rot = jnp.where(first_half, pltpu.roll(xc, LANES - 16, 1), pltpu.roll(xc, 16, 1))
            outs.append(xc * cos + rot * sin)
        return outs

    z_qkv = project(0, o_ga)
    z_ga = project(o_ga, o_f)
    msq_q = head_mean_sq(z_qkv[:, 0:o_k], ATTN_WIDTH)
    msq_k = head_mean_sq(z_qkv[:, o_k:o_v], KV_WIDTH)
    z_f = project(o_f, o_gf)

    vt = z_qkv[:, o_v:o_ga].T.astype(BF16)
    ones = jnp.ones((V_ROWS - HEAD_DIM, vt.shape[1]), BF16)
    for g in range(N_KV_HEADS):
        vt_ref[0, g, 0:HEAD_DIM, :] = vt[g * HEAD_DIM:(g + 1) * HEAD_DIM]
        vt_ref[0, g, HEAD_DIM:V_ROWS, :] = ones
    ga_ref[0] = (z_ga * jax.nn.sigmoid(z_ga)).astype(BF16)

    z_gf = project(o_gf, o_gf + F_WIDTH)
    f = z_f.astype(BF16)
    for t in range(0, F_WIDTH, MXU_DIM):
        for half in (0, F_WIDTH):
            xcs_ref[0, :, half + t:half + t + MXU_DIM] = jnp.dot(
                f[:, t:t + MXU_DIM], wz_ref[t:t + MXU_DIM, half + t:half + t + MXU_DIM],
                preferred_element_type=F32).astype(BF16)

    for c, qc in enumerate(norm_rope(z_qkv[:, 0:o_k], msq_q, gq_ref[...])):
        qt = (qc * scale).T.astype(BF16)
        q_ref[0, 2 * c] = qt[0:HEAD_DIM]
        q_ref[0, 2 * c + 1] = qt[HEAD_DIM:LANES]
    (kc,) = norm_rope(z_qkv[:, o_k:o_v], msq_k, gk_ref[...])
    kc = kc.astype(BF16)
    k_ref[0, 0] = kc[:, 0:HEAD_DIM]
    k_ref[0, 1] = kc[:, HEAD_DIM:LANES]
    gf_ref[0] = (z_gf * jax.nn.sigmoid(z_gf)).astype(BF16)


def _attn_kernel(q_ref, k_ref, vt_ref, o_ref, m_ref, acc_ref, s0_ref, s1_ref, *,
                 q_tile, k_tile, q_chunk):
    seq = k_ref.shape[2]
    block_cols = Q_PER_KV * q_tile
    n_chunks = block_cols // q_chunk
    n_tiles = seq // k_tile
    n_items = (seq // q_tile) * n_tiles
    assert n_tiles % 2 == 0
    m_ref[...] = jnp.full(m_ref.shape, -jnp.inf, F32)
    acc_ref[...] = jnp.zeros(acc_ref.shape, F32)

    def scores(item, c, s_ref):
        h, r = divmod(c * q_chunk, q_tile)
        keys = pl.ds(pl.multiple_of((item % n_tiles) * k_tile, k_tile), k_tile)
        queries = pl.ds(pl.multiple_of((item // n_tiles) * q_tile + r, q_chunk), q_chunk)
        s_ref[c] = jnp.dot(k_ref[0, 0, keys, :], q_ref[0, h, :, queries],
                           preferred_element_type=F32)

    def softmax_pv(item, c, s_ref):
        keys = pl.ds(pl.multiple_of((item % n_tiles) * k_tile, k_tile), k_tile)
        cols = pl.ds(pl.multiple_of((item // n_tiles) * block_cols + c * q_chunk, q_chunk),
                     q_chunk)
        m_old = m_ref[:, cols]
        m_new = jnp.maximum(m_old, jnp.max(s_ref[c], axis=0, keepdims=True))
        alpha = jnp.exp2(m_old - m_new)
        p = jnp.exp2(s_ref[c] - m_new).astype(BF16)
        acc_ref[:, cols] = alpha * acc_ref[:, cols] + jnp.dot(
            vt_ref[0, 0, :, keys], p, preferred_element_type=F32)
        m_ref[:, cols] = m_new

    def step(item, cur_ref, nxt_ref, prefetch=True):
        for c in range(n_chunks):
            if prefetch:
                scores(item + 1, c, nxt_ref)
            softmax_pv(item, c, cur_ref)

    for c in range(n_chunks):
        scores(0, c, s0_ref)

    def body(i, carry):
        step(2 * i, s0_ref, s1_ref)
        step(2 * i + 1, s1_ref, s0_ref)
        return carry

    lax.fori_loop(0, n_items // 2 - 1, body, 0)
    step(n_items - 2, s0_ref, s1_ref)
    step(n_items - 1, s1_ref, s0_ref, prefetch=False)

    pad = jnp.zeros((LANES - HEAD_DIM, q_tile), F32)
    for blk in range(seq // q_tile):
        for h in range(Q_PER_KV):
            col0 = blk * block_cols + h * q_tile
            acc = acc_ref[:, col0:col0 + q_tile]
            o = acc[0:HEAD_DIM] / acc[HEAD_DIM:HEAD_DIM + 1]
            ot = jnp.concatenate([o, pad], axis=0).T
            o_ref[0, blk * q_tile:(blk + 1) * q_tile, h * HEAD_DIM:(h + 1) * HEAD_DIM] = (
                ot[:, 0:HEAD_DIM].astype(BF16))


def _dft_kernel(ct_ref, st_ref, alt_ref, rev_ref, x_ref, y_ref, pq_ref, u_ref):
    seq = x_ref.shape[1]
    half = seq // 2
    nb = half // REV_BLOCK
    rev = rev_ref[...]

    def is_row0(width):
        return lax.broadcasted_iota(jnp.int32, (REV_BLOCK, width), 0) == 0

    for i in range(nb):
        lo = slice(i * REV_BLOCK, (i + 1) * REV_BLOCK)
        mirror = jnp.dot(rev, x_ref[0, (2 * nb - 1 - i) * REV_BLOCK:(2 * nb - i) * REV_BLOCK, :],
                         preferred_element_type=F32)
        first = ((2 * nb - i) * REV_BLOCK) % seq
        mirror = jnp.where(is_row0(2 * F_WIDTH), x_ref[0, first:first + 1, :].astype(F32), mirror)
        xb = x_ref[0, lo, :].astype(F32)
        pq_ref[lo, 0:F_WIDTH] = (xb[:, 0:F_WIDTH] + mirror[:, 0:F_WIDTH]).astype(BF16)
        pq_ref[lo, F_WIDTH:] = (xb[:, F_WIDTH:] - mirror[:, F_WIDTH:]).astype(BF16)

    p_mid = x_ref[0, half:half + 1, 0:F_WIDTH].astype(F32)
    odd_k = (lax.broadcasted_iota(jnp.int32, (DFT_TILE, F_WIDTH), 0) & 1) == 1
    p_mid_signed = jnp.where(odd_k, -p_mid, p_mid)
    for kb in range(half // DFT_TILE):
        rows = slice(kb * DFT_TILE, (kb + 1) * DFT_TILE)
        a = jnp.dot(ct_ref[rows, :], pq_ref[:, 0:F_WIDTH], preferred_element_type=F32)
        a = a + p_mid_signed
        b = jnp.dot(st_ref[rows, :], pq_ref[:, F_WIDTH:], preferred_element_type=F32)
        y_ref[0, rows, :] = (a - b).astype(BF16)
        u_ref[rows, :] = (a + b).astype(BF16)

    y_mid = jnp.dot(alt_ref[...], pq_ref[:, 0:F_WIDTH], preferred_element_type=F32)[0:1] + p_mid

    for i in range(nb):
        mirror = jnp.dot(rev, u_ref[(nb - 1 - i) * REV_BLOCK:(nb - i) * REV_BLOCK, :],
                         preferred_element_type=F32)
        if i == 0:
            first_row = y_mid
        else:
            first_row = u_ref[(nb - i) * REV_BLOCK:(nb - i) * REV_BLOCK + 1, :].astype(F32)
        y_ref[0, half + i * REV_BLOCK:half + (i + 1) * REV_BLOCK, :] = jnp.where(
            is_row0(F_WIDTH), first_row, mirror).astype(BF16)


def _proj_out_kernel(x_ref, a_ref, ga_ref, fm_ref, gf_ref, p_ref, wo_ref, wg_ref, wp_ref,
                     gple_ref, gfin_ref, y_ref, *, final_norm):
    sub = x_ref.shape[1] // ROW_SPLIT
    halves = [slice(r * sub, (r + 1) * sub) for r in range(ROW_SPLIT)]
    hs, ples = [], []
    for rows in halves:
        mix_a = (a_ref[0, rows, :].astype(F32) * ga_ref[0, rows, :].astype(F32)).astype(BF16)
        mix_f = (fm_ref[0, rows, :].astype(F32) * gf_ref[0, rows, :].astype(F32)).astype(BF16)
        h = x_ref[0, rows, :] + jnp.dot(mix_a, wo_ref[0:ATTN_WIDTH, :],
                                        preferred_element_type=F32)
        hs.append(h + jnp.dot(mix_f, wo_ref[ATTN_WIDTH:ATTN_WIDTH + F_WIDTH, :],
                              preferred_element_type=F32))
        ples.append(jnp.dot(p_ref[0, rows, :].astype(BF16), wp_ref[...],
                            preferred_element_type=F32))
    for rows, h, ple in zip(halves, hs, ples):
        hn = _rmsnorm(h, gple_ref[...]).astype(BF16)
        gate = jax.nn.sigmoid(jnp.dot(hn, wg_ref[...], preferred_element_type=F32))
        h = h + ple * gate
        if final_norm:
            h = _rmsnorm(h, gfin_ref[...])
        y_ref[0, rows, :] = h


def _const_spec(shape):
    return pl.BlockSpec(shape, lambda *_: (0,) * len(shape))


def _fourier_weights(w_fmix, seq_len):
    cs = jnp.asarray(_chan_dft_table(seq_len))
    out = pl.pallas_call(
        _wz_kernel,
        out_shape=jax.ShapeDtypeStruct((N_FGROUPS, 2 * FGROUP_DIM, FGROUP_DIM), F32),
        name="fourier_weights",
    )(cs, w_fmix)
    eye = jnp.eye(N_FGROUPS, dtype=F32)

    def block_diag(w):
        return (eye[:, None, :, None] * w[:, :, None, :]).reshape(F_WIDTH, F_WIDTH)

    return jnp.concatenate([block_diag(out[:, :FGROUP_DIM]), block_diag(out[:, FGROUP_DIM:])],
                           axis=1).astype(BF16)


def _layer(x, p, w, *, final_norm):
    batch, seq, d_model = x.shape
    in_width = w["w_in"].shape[1]
    nt = seq // TOK_TILE
    tok = lambda width: pl.BlockSpec((1, TOK_TILE, width), lambda b, t: (b, t, 0))

    q, k, vt, ga, xcs, gf = pl.pallas_call(
        _proj_in_kernel,
        grid=(batch, nt),
        in_specs=[
            tok(d_model),
            _const_spec((d_model, in_width)),
            _const_spec((1, d_model)),
            _const_spec((1, ATTN_WIDTH)),
            _const_spec((1, KV_WIDTH)),
            pl.BlockSpec((TOK_TILE, LANES), lambda b, t: (t, 0)),
            pl.BlockSpec((TOK_TILE, LANES), lambda b, t: (t, 0)),
            _const_spec((ATTN_WIDTH, ATTN_WIDTH)),
            _const_spec((F_WIDTH, 2 * F_WIDTH)),
        ],
        out_specs=[
            pl.BlockSpec((1, N_HEADS, HEAD_DIM, TOK_TILE), lambda b, t: (b, 0, 0, t)),
            pl.BlockSpec((1, N_KV_HEADS, TOK_TILE, HEAD_DIM), lambda b, t: (b, 0, t, 0)),
            pl.BlockSpec((1, N_KV_HEADS, V_ROWS, TOK_TILE), lambda b, t: (b, 0, 0, t)),
            tok(ATTN_WIDTH),
            tok(2 * F_WIDTH),
            tok(F_WIDTH),
        ],
        out_shape=[
            jax.ShapeDtypeStruct((batch, N_HEADS, HEAD_DIM, seq), BF16),
            jax.ShapeDtypeStruct((batch, N_KV_HEADS, seq, HEAD_DIM), BF16),
            jax.ShapeDtypeStruct((batch, N_KV_HEADS, V_ROWS, seq), BF16),
            jax.ShapeDtypeStruct((batch, seq, ATTN_WIDTH), BF16),
            jax.ShapeDtypeStruct((batch, seq, 2 * F_WIDTH), BF16),
            jax.ShapeDtypeStruct((batch, seq, F_WIDTH), BF16),
        ],
        compiler_params=_params("parallel", "parallel"),
        name="proj_in",
    )(x, w["w_in"], w["g_norm"], w["g_q"], w["g_k"], w["cos"], w["sin"], w["bd"], w["wz"])

    kv_width = Q_PER_KV * HEAD_DIM
    a = pl.pallas_call(
        functools.partial(_attn_kernel, q_tile=Q_TILE, k_tile=K_TILE, q_chunk=Q_CHUNK),
        grid=(batch, N_KV_HEADS),
        in_specs=[
            pl.BlockSpec((1, Q_PER_KV, HEAD_DIM, seq), lambda b, g: (b, g, 0, 0)),
            pl.BlockSpec((1, 1, seq, HEAD_DIM), lambda b, g: (b, g, 0, 0)),
            pl.BlockSpec((1, 1, V_ROWS, seq), lambda b, g: (b, g, 0, 0)),
        ],
        out_specs=pl.BlockSpec((1, seq, kv_width), lambda b, g: (b, 0, g)),
        out_shape=jax.ShapeDtypeStruct((batch, seq, ATTN_WIDTH), BF16),
        scratch_shapes=[pltpu.VMEM((1, Q_PER_KV * seq), F32),
                        pltpu.VMEM((V_ROWS, Q_PER_KV * seq), F32),
                        pltpu.VMEM((Q_PER_KV * Q_TILE // Q_CHUNK, K_TILE, Q_CHUNK), F32),
                        pltpu.VMEM((Q_PER_KV * Q_TILE // Q_CHUNK, K_TILE, Q_CHUNK), F32)],
        compiler_params=_params("parallel", "parallel"),
        name="attn",
    )(q, k, vt)

    half = seq // 2
    resident = lambda shape: pl.BlockSpec(shape, lambda b: (0,) * len(shape),
                                          pipeline_mode=pl.Buffered(1))
    fm = pl.pallas_call(
        _dft_kernel,
        grid=(batch,),
        in_specs=[
            resident((half, half)),
            resident((half, half)),
            resident(w["alt"].shape),
            resident((REV_BLOCK, REV_BLOCK)),
            pl.BlockSpec((1, seq, 2 * F_WIDTH), lambda b: (b, 0, 0)),
        ],
        out_specs=pl.BlockSpec((1, seq, F_WIDTH), lambda b: (b, 0, 0)),
        out_shape=jax.ShapeDtypeStruct((batch, seq, F_WIDTH), BF16),
        scratch_shapes=[pltpu.VMEM((half, 2 * F_WIDTH), BF16),
                        pltpu.VMEM((half, F_WIDTH), BF16)],
        compiler_params=_params("parallel"),
        name="dft",
    )(w["ct"], w["st"], w["alt"], w["rev"], xcs)

    ple_dim = p.shape[-1]
    return pl.pallas_call(
        functools.partial(_proj_out_kernel, final_norm=final_norm),
        grid=(batch, nt),
        in_specs=[
            tok(d_model), tok(ATTN_WIDTH), tok(ATTN_WIDTH), tok(F_WIDTH), tok(F_WIDTH),
            tok(ple_dim),
            _const_spec((ATTN_WIDTH + F_WIDTH, d_model)),
            _const_spec((d_model, d_model)),
            _const_spec((ple_dim, d_model)),
            _const_spec((1, d_model)),
            _const_spec((1, d_model)),
        ],
        out_specs=tok(d_model),
        out_shape=jax.ShapeDtypeStruct((batch, seq, d_model), F32),
        compiler_params=_params("parallel", "parallel"),
        name="proj_out",
    )(x, a, ga, fm, gf, p, w["w_out"], w["w_ple_gate"], w["w_ple"], w["g_ple"], w["g_final"])


def _layer_weights(i, seq, g_norm, w_in, g_q, g_k, w_fmix, w_out, g_ple, w_ple_gate, w_ple, g_final):
    cos, sin = _rope_tables(seq)
    ct, st, alt = _seq_dft_tables(seq)
    heads = np.arange(ATTN_WIDTH) // HEAD_DIM
    bd = jnp.asarray((heads[:, None] == heads[None, :]).astype(np.float32) / HEAD_DIM, dtype=BF16)
    return {
        "w_in": w_in[i].astype(BF16),
        "g_norm": g_norm[i][None, :],
        "g_q": jnp.tile(g_q[i], N_HEADS)[None, :],
        "g_k": jnp.tile(g_k[i], N_KV_HEADS)[None, :],
        "cos": cos, "sin": sin, "bd": bd,
        "wz": _fourier_weights(w_fmix[i], seq),
        "ct": ct, "st": st, "alt": alt, "rev": _shifted_reversal(REV_BLOCK),
        "w_out": w_out[i].astype(BF16),
        "w_ple_gate": w_ple_gate[i].astype(BF16),
        "w_ple": w_ple[i].astype(BF16),
        "g_ple": g_ple[i][None, :],
        "g_final": g_final[None, :],
    }


def _trunk(x, p, layer_weights):
    h = x
    depth = len(layer_weights)
    for i, w in enumerate(layer_weights):
        h = _layer(h, p[i], w, final_norm=(i == depth - 1))
    return h


def kernel(x_prompt, x_sample, p_prompt, p_sample, g_norm, w_in, g_q, g_k, w_fmix, w_out,
           g_ple, w_ple_gate, w_ple, g_final):
    depth = g_norm.shape[0]
    outs = []
    weights_by_seq = {}
    for x, p in ((x_prompt, p_prompt), (x_sample, p_sample)):
        seq = x.shape[1]
        if seq not in weights_by_seq:
            weights_by_seq[seq] = [
                _layer_weights(i, seq, g_norm, w_in, g_q, g_k, w_fmix, w_out, g_ple,
                               w_ple_gate, w_ple, g_final) for i in range(depth)]
        outs.append(_trunk(x, p, weights_by_seq[seq]))
    return tuple(outs)
```

```python
import functools

import jax
import jax.numpy as jnp
import numpy as np
from jax import lax
from jax.experimental import pallas as pl
from jax.experimental.pallas import tpu as pltpu

HEAD_DIM = 64
N_HEADS = 8
N_KV_HEADS = 2
Q_PER_KV = N_HEADS // N_KV_HEADS
ATTN_WIDTH = N_HEADS * HEAD_DIM
KV_WIDTH = N_KV_HEADS * HEAD_DIM
N_FGROUPS = 8
FGROUP_DIM = 64
F_WIDTH = N_FGROUPS * FGROUP_DIM
GRID_W = 64
ROPE_THETA = 10000.0
EPS = 1e-6

LANES = 128
MXU_DIM = 256
ROW_SPLIT = 2
V_ROWS = 80
VMEM_LIMIT_BYTES = 56 * 1024 * 1024

TOK_TILE = 1024
Q_TILE = 1024
K_TILE = 512
Q_CHUNK = 256
DFT_TILE = 512
REV_BLOCK = 256

BF16 = jnp.bfloat16
F32 = jnp.float32


def _params(*semantics):
    return pltpu.CompilerParams(dimension_semantics=semantics,
                                vmem_limit_bytes=VMEM_LIMIT_BYTES)


def _rope_tables(seq_len):
    rows = seq_len // GRID_W
    row = jnp.repeat(jnp.arange(rows, dtype=F32), GRID_W)
    col = jnp.tile(jnp.arange(GRID_W, dtype=F32), rows)
    n_freq = HEAD_DIM // 4
    inv_freq = ROPE_THETA ** (-jnp.arange(n_freq, dtype=F32) / n_freq)
    ang_r = row[:, None] * inv_freq[None, :]
    ang_c = col[:, None] * inv_freq[None, :]
    ang = jnp.concatenate([ang_r, ang_r, ang_c, ang_c], axis=-1)
    sign = jnp.where((jnp.arange(HEAD_DIM) % 32) < 16, -1.0, 1.0).astype(F32)
    cos = jnp.cos(ang)
    sin = jnp.sin(ang) * sign[None, :]
    reps = LANES // HEAD_DIM
    return jnp.tile(cos, (1, reps)), jnp.tile(sin, (1, reps))


def _seq_dft_tables(seq_len):
    half = seq_len // 2
    k = jnp.arange(half, dtype=jnp.int32)[:, None]
    s1 = jnp.arange(half // GRID_W, dtype=jnp.int32)[None, :]
    s2 = jnp.arange(GRID_W, dtype=jnp.int32)[None, :]
    step = 2.0 * np.pi / seq_len
    ang_hi = ((k * s1 * GRID_W) % seq_len).astype(F32) * step
    ang_lo = ((k * s2) % seq_len).astype(F32) * step
    ch, sh = jnp.cos(ang_hi)[:, :, None], jnp.sin(ang_hi)[:, :, None]
    cl, sl = jnp.cos(ang_lo)[:, None, :], jnp.sin(ang_lo)[:, None, :]
    col_weight = np.ones((1, half), np.float32)
    col_weight[0, 0] = 0.5
    ct = (ch * cl - sh * sl).reshape(half, half) * col_weight
    st = (sh * cl + ch * sl).reshape(half, half)
    alt = np.zeros((16, half), np.float32)
    alt[0] = (1.0 - 2.0 * (np.arange(half) % 2)) * col_weight[0]
    return ct.astype(BF16), st.astype(BF16), jnp.asarray(alt, dtype=BF16)


def _shifted_reversal(n):
    r = np.zeros((n, n), np.float32)
    t = np.arange(1, n)
    r[t, n - t] = 1.0
    return jnp.asarray(r, dtype=BF16)


def _chan_dft_table(seq_len):
    idx = np.arange(FGROUP_DIM, dtype=np.int64)
    ang = ((idx[:, None] * idx[None, :]) % FGROUP_DIM).astype(np.float64) * (2.0 * np.pi / FGROUP_DIM)
    scale = 1.0 / np.sqrt(float(seq_len) * FGROUP_DIM)
    return (np.concatenate([np.cos(ang), np.sin(ang)], axis=0) * scale).astype(np.float32)


def _wz_kernel(cs_ref, w_ref, o_ref):
    for g in range(N_FGROUPS):
        o_ref[g] = jnp.dot(cs_ref[...], w_ref[g], precision=lax.Precision.HIGHEST,
                           preferred_element_type=F32)


def _rmsnorm(x, g):
    return x * lax.rsqrt(jnp.mean(x * x, axis=-1, keepdims=True) + EPS) * g


def _proj_in_kernel(x_ref, w_ref, gn_ref, gq_ref, gk_ref, cos_ref, sin_ref, bd_ref, wz_ref,
                    q_ref, k_ref, vt_ref, ga_ref, xcs_ref, gf_ref):
    o_k = ATTN_WIDTH
    o_v = o_k + KV_WIDTH
    o_ga = o_v + KV_WIDTH
    o_f = o_ga + ATTN_WIDTH
    o_gf = o_f + F_WIDTH
    lane = lax.broadcasted_iota(jnp.int32, (1, LANES), 1)
    first_half = (lane % 32) < 16
    scale = HEAD_DIM ** -0.5 * np.log2(np.e)
    cos = cos_ref[...]
    sin = sin_ref[...]

    u = _rmsnorm(x_ref[0], gn_ref[...]).astype(BF16)

    def project(lo, hi):
        return jnp.dot(u, w_ref[:, lo:hi], preferred_element_type=F32)

    def head_mean_sq(zz, width):
        sq = (zz * zz).astype(BF16)
        step = min(width, MXU_DIM)
        return jnp.concatenate(
            [jnp.dot(sq[:, t:t + step], bd_ref[t:t + step, t:t + step],
                     preferred_element_type=F32) for t in range(0, width, step)], axis=1)

    def norm_rope(zz, msq, g):
        n = zz * lax.rsqrt(msq + EPS) * g
        outs = []
        for c in range(zz.shape[1] // LANES):
            xc = n[:, c * LANES:(c + 1) * LANES]
            rot = jnp.where(first_half, pltpu.roll(xc, LANES - 16, 1), pltpu.roll(xc, 16, 1))
            outs.append(xc * cos + rot * sin)
        return outs

    z_qkv = project(0, o_ga)
    z_ga = project(o_ga, o_f)
    msq_q = head_mean_sq(z_qkv[:, 0:o_k], ATTN_WIDTH)
    msq_k = head_mean_sq(z_qkv[:, o_k:o_v], KV_WIDTH)
    z_f = project(o_f, o_gf)

    vt = z_qkv[:, o_v:o_ga].T.astype(BF16)
    ones = jnp.ones((V_ROWS - HEAD_DIM, vt.shape[1]), BF16)
    for g in range(N_KV_HEADS):
        vt_ref[0, g, 0:HEAD_DIM, :] = vt[g * HEAD_DIM:(g + 1) * HEAD_DIM]
        vt_ref[0, g, HEAD_DIM:V_ROWS, :] = ones
    ga_ref[0] = (z_ga * jax.nn.sigmoid(z_ga)).astype(BF16)

    z_gf = project(o_gf, o_gf + F_WIDTH)
    f = z_f.astype(BF16)
    for t in range(0, F_WIDTH, MXU_DIM):
        for half in (0, F_WIDTH):
            xcs_ref[0, :, half + t:half + t + MXU_DIM] = jnp.dot(
                f[:, t:t + MXU_DIM], wz_ref[t:t + MXU_DIM, half + t:half + t + MXU_DIM],
                preferred_element_type=F32).astype(BF16)

    for c, qc in enumerate(norm_rope(z_qkv[:, 0:o_k], msq_q, gq_ref[...])):
        qt = (qc * scale).T.astype(BF16)
        q_ref[0, 2 * c] = qt[0:HEAD_DIM]
        q_ref[0, 2 * c + 1] = qt[HEAD_DIM:LANES]
    (kc,) = norm_rope(z_qkv[:, o_k:o_v], msq_k, gk_ref[...])
    kc = kc.astype(BF16)
    k_ref[0, 0] = kc[:, 0:HEAD_DIM]
    k_ref[0, 1] = kc[:, HEAD_DIM:LANES]
    gf_ref[0] = (z_gf * jax.nn.sigmoid(z_gf)).astype(BF16)


def _attn_kernel(q_ref, k_ref, vt_ref, o_ref, m_ref, acc_ref, s0_ref, s1_ref, t0_ref, t1_ref, *,
                 q_tile, k_tile, q_chunk):
    seq = k_ref.shape[2]
    block_cols = Q_PER_KV * q_tile
    n_chunks = block_cols // q_chunk
    n_tiles = seq // k_tile
    n_items = (seq // q_tile) * n_tiles
    assert n_tiles % 2 == 0
    m_ref[...] = jnp.full(m_ref.shape, -jnp.inf, F32)
    acc_ref[...] = jnp.zeros(acc_ref.shape, F32)

    buffers = ((s0_ref, t0_ref), (s1_ref, t1_ref))

    def scores(item, c, buf):
        s_ref, t_ref = buf
        h, r = divmod(c * q_chunk, q_tile)
        keys = pl.ds(pl.multiple_of((item % n_tiles) * k_tile, k_tile), k_tile)
        queries = pl.ds(pl.multiple_of((item // n_tiles) * q_tile + r, q_chunk), q_chunk)
        s = jnp.dot(k_ref[0, 0, keys, :], q_ref[0, h, :, queries],
                    preferred_element_type=F32)
        s_ref[c] = s
        t_ref[c] = jnp.max(s, axis=0, keepdims=True)

    def softmax_pv(item, c, buf):
        s_ref, t_ref = buf
        keys = pl.ds(pl.multiple_of((item % n_tiles) * k_tile, k_tile), k_tile)
        cols = pl.ds(pl.multiple_of((item // n_tiles) * block_cols + c * q_chunk, q_chunk),
                     q_chunk)
        m_old = m_ref[:, cols]
        m_new = jnp.maximum(m_old, t_ref[c])
        alpha = jnp.exp2(m_old - m_new)
        p = jnp.exp2(s_ref[c] - m_new).astype(BF16)
        acc_ref[:, cols] = alpha * acc_ref[:, cols] + jnp.dot(
            vt_ref[0, 0, :, keys], p, preferred_element_type=F32)
        m_ref[:, cols] = m_new

    def step(item, parity, prefetch=True):
        for c in range(n_chunks):
            if prefetch:
                scores(item + 1, c, buffers[1 - parity])
            softmax_pv(item, c, buffers[parity])

    for c in range(n_chunks):
        scores(0, c, buffers[0])

    def body(i, carry):
        step(2 * i, 0)
        step(2 * i + 1, 1)
        return carry

    lax.fori_loop(0, n_items // 2 - 1, body, 0)
    step(n_items - 2, 0)
    step(n_items - 1, 1, prefetch=False)

    pad = jnp.zeros((LANES - HEAD_DIM, q_tile), F32)
    for blk in range(seq // q_tile):
        for h in range(Q_PER_KV):
            col0 = blk * block_cols + h * q_tile
            acc = acc_ref[:, col0:col0 + q_tile]
            o = acc[0:HEAD_DIM] / acc[HEAD_DIM:HEAD_DIM + 1]
            ot = jnp.concatenate([o, pad], axis=0).T
            o_ref[0, blk * q_tile:(blk + 1) * q_tile, h * HEAD_DIM:(h + 1) * HEAD_DIM] = (
                ot[:, 0:HEAD_DIM].astype(BF16))


def _dft_kernel(ct_ref, st_ref, alt_ref, rev_ref, x_ref, y_ref, pq_ref, u_ref):
    seq = x_ref.shape[1]
    half = seq // 2
    nb = half // REV_BLOCK
    rev = rev_ref[...]

    def is_row0(width):
        return lax.broadcasted_iota(jnp.int32, (REV_BLOCK, width), 0) == 0

    for i in range(nb):
        lo = slice(i * REV_BLOCK, (i + 1) * REV_BLOCK)
        mirror = jnp.dot(rev, x_ref[0, (2 * nb - 1 - i) * REV_BLOCK:(2 * nb - i) * REV_BLOCK, :],
                         preferred_element_type=F32)
        first = ((2 * nb - i) * REV_BLOCK) % seq
        mirror = jnp.where(is_row0(2 * F_WIDTH), x_ref[0, first:first + 1, :].astype(F32), mirror)
        xb = x_ref[0, lo, :].astype(F32)
        pq_ref[lo, 0:F_WIDTH] = (xb[:, 0:F_WIDTH] + mirror[:, 0:F_WIDTH]).astype(BF16)
        pq_ref[lo, F_WIDTH:] = (xb[:, F_WIDTH:] - mirror[:, F_WIDTH:]).astype(BF16)

    p_mid = x_ref[0, half:half + 1, 0:F_WIDTH].astype(F32)
    odd_k = (lax.broadcasted_iota(jnp.int32, (DFT_TILE, F_WIDTH), 0) & 1) == 1
    p_mid_signed = jnp.where(odd_k, -p_mid, p_mid)
    for kb in range(half // DFT_TILE):
        rows = slice(kb * DFT_TILE, (kb + 1) * DFT_TILE)
        a = jnp.dot(ct_ref[rows, :], pq_ref[:, 0:F_WIDTH], preferred_element_type=F32)
        a = a + p_mid_signed
        b = jnp.dot(st_ref[rows, :], pq_ref[:, F_WIDTH:], preferred_element_type=F32)
        y_ref[0, rows, :] = (a - b).astype(BF16)
        u_ref[rows, :] = (a + b).astype(BF16)

    y_mid = jnp.dot(alt_ref[...], pq_ref[:, 0:F_WIDTH], preferred_element_type=F32)[0:1] + p_mid

    for i in range(nb):
        mirror = jnp.dot(rev, u_ref[(nb - 1 - i) * REV_BLOCK:(nb - i) * REV_BLOCK, :],
                         preferred_element_type=F32)
        if i == 0:
            first_row = y_mid
        else:
            first_row = u_ref[(nb - i) * REV_BLOCK:(nb - i) * REV_BLOCK + 1, :].astype(F32)
        y_ref[0, half + i * REV_BLOCK:half + (i + 1) * REV_BLOCK, :] = jnp.where(
            is_row0(F_WIDTH), first_row, mirror).astype(BF16)


def _proj_out_kernel(x_ref, a_ref, ga_ref, fm_ref, gf_ref, p_ref, wo_ref, wg_ref, wp_ref,
                     gple_ref, gfin_ref, y_ref, *, final_norm):
    sub = x_ref.shape[1] // ROW_SPLIT
    halves = [slice(r * sub, (r + 1) * sub) for r in range(ROW_SPLIT)]
    hs, ples = [], []
    for rows in halves:
        mix_a = (a_ref[0, rows, :].astype(F32) * ga_ref[0, rows, :].astype(F32)).astype(BF16)
        mix_f = (fm_ref[0, rows, :].astype(F32) * gf_ref[0, rows, :].astype(F32)).astype(BF16)
        h = x_ref[0, rows, :] + jnp.dot(mix_a, wo_ref[0:ATTN_WIDTH, :],
                                        preferred_element_type=F32)
        hs.append(h + jnp.dot(mix_f, wo_ref[ATTN_WIDTH:ATTN_WIDTH + F_WIDTH, :],
                              preferred_element_type=F32))
        ples.append(jnp.dot(p_ref[0, rows, :].astype(BF16), wp_ref[...],
                            preferred_element_type=F32))
    for rows, h, ple in zip(halves, hs, ples):
        hn = _rmsnorm(h, gple_ref[...]).astype(BF16)
        gate = jax.nn.sigmoid(jnp.dot(hn, wg_ref[...], preferred_element_type=F32))
        h = h + ple * gate
        if final_norm:
            h = _rmsnorm(h, gfin_ref[...])
        y_ref[0, rows, :] = h


def _const_spec(shape):
    return pl.BlockSpec(shape, lambda *_: (0,) * len(shape))


def _fourier_weights(w_fmix, seq_len):
    cs = jnp.asarray(_chan_dft_table(seq_len))
    out = pl.pallas_call(
        _wz_kernel,
        out_shape=jax.ShapeDtypeStruct((N_FGROUPS, 2 * FGROUP_DIM, FGROUP_DIM), F32),
        name="fourier_weights",
    )(cs, w_fmix)
    eye = jnp.eye(N_FGROUPS, dtype=F32)

    def block_diag(w):
        return (eye[:, None, :, None] * w[:, :, None, :]).reshape(F_WIDTH, F_WIDTH)

    return jnp.concatenate([block_diag(out[:, :FGROUP_DIM]), block_diag(out[:, FGROUP_DIM:])],
                           axis=1).astype(BF16)


def _layer(x, p, w, *, final_norm):
    batch, seq, d_model = x.shape
    in_width = w["w_in"].shape[1]
    nt = seq // TOK_TILE
    tok = lambda width: pl.BlockSpec((1, TOK_TILE, width), lambda b, t: (b, t, 0))

    q, k, vt, ga, xcs, gf = pl.pallas_call(
        _proj_in_kernel,
        grid=(batch, nt),
        in_specs=[
            tok(d_model),
            _const_spec((d_model, in_width)),
            _const_spec((1, d_model)),
            _const_spec((1, ATTN_WIDTH)),
            _const_spec((1, KV_WIDTH)),
            pl.BlockSpec((TOK_TILE, LANES), lambda b, t: (t, 0)),
            pl.BlockSpec((TOK_TILE, LANES), lambda b, t: (t, 0)),
            _const_spec((ATTN_WIDTH, ATTN_WIDTH)),
            _const_spec((F_WIDTH, 2 * F_WIDTH)),
        ],
        out_specs=[
            pl.BlockSpec((1, N_HEADS, HEAD_DIM, TOK_TILE), lambda b, t: (b, 0, 0, t)),
            pl.BlockSpec((1, N_KV_HEADS, TOK_TILE, HEAD_DIM), lambda b, t: (b, 0, t, 0)),
            pl.BlockSpec((1, N_KV_HEADS, V_ROWS, TOK_TILE), lambda b, t: (b, 0, 0, t)),
            tok(ATTN_WIDTH),
            tok(2 * F_WIDTH),
            tok(F_WIDTH),
        ],
        out_shape=[
            jax.ShapeDtypeStruct((batch, N_HEADS, HEAD_DIM, seq), BF16),
            jax.ShapeDtypeStruct((batch, N_KV_HEADS, seq, HEAD_DIM), BF16),
            jax.ShapeDtypeStruct((batch, N_KV_HEADS, V_ROWS, seq), BF16),
            jax.ShapeDtypeStruct((batch, seq, ATTN_WIDTH), BF16),
            jax.ShapeDtypeStruct((batch, seq, 2 * F_WIDTH), BF16),
            jax.ShapeDtypeStruct((batch, seq, F_WIDTH), BF16),
        ],
        compiler_params=_params("parallel", "parallel"),
        name="proj_in",
    )(x, w["w_in"], w["g_norm"], w["g_q"], w["g_k"], w["cos"], w["sin"], w["bd"], w["wz"])

    kv_width = Q_PER_KV * HEAD_DIM
    a = pl.pallas_call(
        functools.partial(_attn_kernel, q_tile=Q_TILE, k_tile=K_TILE, q_chunk=Q_CHUNK),
        grid=(batch, N_KV_HEADS),
        in_specs=[
            pl.BlockSpec((1, Q_PER_KV, HEAD_DIM, seq), lambda b, g: (b, g, 0, 0)),
            pl.BlockSpec((1, 1, seq, HEAD_DIM), lambda b, g: (b, g, 0, 0)),
            pl.BlockSpec((1, 1, V_ROWS, seq), lambda b, g: (b, g, 0, 0)),
        ],
        out_specs=pl.BlockSpec((1, seq, kv_width), lambda b, g: (b, 0, g)),
        out_shape=jax.ShapeDtypeStruct((batch, seq, ATTN_WIDTH), BF16),
        scratch_shapes=[pltpu.VMEM((1, Q_PER_KV * seq), F32),
                        pltpu.VMEM((V_ROWS, Q_PER_KV * seq), F32),
                        pltpu.VMEM((Q_PER_KV * Q_TILE // Q_CHUNK, K_TILE, Q_CHUNK), F32),
                        pltpu.VMEM((Q_PER_KV * Q_TILE // Q_CHUNK, K_TILE, Q_CHUNK), F32),
                        pltpu.VMEM((Q_PER_KV * Q_TILE // Q_CHUNK, 1, Q_CHUNK), F32),
                        pltpu.VMEM((Q_PER_KV * Q_TILE // Q_CHUNK, 1, Q_CHUNK), F32)],
        compiler_params=_params("parallel", "parallel"),
        name="attn",
    )(q, k, vt)

    half = seq // 2
    resident = lambda shape: pl.BlockSpec(shape, lambda b: (0,) * len(shape),
                                          pipeline_mode=pl.Buffered(1))
    fm = pl.pallas_call(
        _dft_kernel,
        grid=(batch,),
        in_specs=[
            resident((half, half)),
            resident((half, half)),
            resident(w["alt"].shape),
            resident((REV_BLOCK, REV_BLOCK)),
            pl.BlockSpec((1, seq, 2 * F_WIDTH), lambda b: (b, 0, 0)),
        ],
        out_specs=pl.BlockSpec((1, seq, F_WIDTH), lambda b: (b, 0, 0)),
        out_shape=jax.ShapeDtypeStruct((batch, seq, F_WIDTH), BF16),
        scratch_shapes=[pltpu.VMEM((half, 2 * F_WIDTH), BF16),
                        pltpu.VMEM((half, F_WIDTH), BF16)],
        compiler_params=_params("parallel"),
        name="dft",
    )(w["ct"], w["st"], w["alt"], w["rev"], xcs)

    ple_dim = p.shape[-1]
    return pl.pallas_call(
        functools.partial(_proj_out_kernel, final_norm=final_norm),
        grid=(batch, nt),
        in_specs=[
            tok(d_model), tok(ATTN_WIDTH), tok(ATTN_WIDTH), tok(F_WIDTH), tok(F_WIDTH),
            tok(ple_dim),
            _const_spec((ATTN_WIDTH + F_WIDTH, d_model)),
            _const_spec((d_model, d_model)),
            _const_spec((ple_dim, d_model)),
            _const_spec((1, d_model)),
            _const_spec((1, d_model)),
        ],
        out_specs=tok(d_model),
        out_shape=jax.ShapeDtypeStruct((batch, seq, d_model), F32),
        compiler_params=_params("parallel", "parallel"),
        name="proj_out",
    )(x, a, ga, fm, gf, p, w["w_out"], w["w_ple_gate"], w["w_ple"], w["g_ple"], w["g_final"])


def _layer_weights(i, seq, g_norm, w_in, g_q, g_k, w_fmix, w_out, g_ple, w_ple_gate, w_ple, g_final):
    cos, sin = _rope_tables(seq)
    ct, st, alt = _seq_dft_tables(seq)
    heads = np.arange(ATTN_WIDTH) // HEAD_DIM
    bd = jnp.asarray((heads[:, None] == heads[None, :]).astype(np.float32) / HEAD_DIM, dtype=BF16)
    return {
        "w_in": w_in[i].astype(BF16),
        "g_norm": g_norm[i][None, :],
        "g_q": jnp.tile(g_q[i], N_HEADS)[None, :],
        "g_k": jnp.tile(g_k[i], N_KV_HEADS)[None, :],
        "cos": cos, "sin": sin, "bd": bd,
        "wz": _fourier_weights(w_fmix[i], seq),
        "ct": ct, "st": st, "alt": alt, "rev": _shifted_reversal(REV_BLOCK),
        "w_out": w_out[i].astype(BF16),
        "w_ple_gate": w_ple_gate[i].astype(BF16),
        "w_ple": w_ple[i].astype(BF16),
        "g_ple": g_ple[i][None, :],
        "g_final": g_final[None, :],
    }


def _trunk(x, p, layer_weights):
    h = x
    depth = len(layer_weights)
    for i, w in enumerate(layer_weights):
        h = _layer(h, p[i], w, final_norm=(i == depth - 1))
    return h


def kernel(x_prompt, x_sample, p_prompt, p_sample, g_norm, w_in, g_q, g_k, w_fmix, w_out,
           g_ple, w_ple_gate, w_ple, g_final):
    depth = g_norm.shape[0]
    outs = []
    weights_by_seq = {}
    for x, p in ((x_prompt, p_prompt), (x_sample, p_sample)):
        seq = x.shape[1]
        if seq not in weights_by_seq:
            weights_by_seq[seq] = [
                _layer_weights(i, seq, g_norm, w_in, g_q, g_k, w_fmix, w_out, g_ple,
                               w_ple_gate, w_ple, g_final) for i in range(depth)]
        outs.append(_trunk(x, p, weights_by_seq[seq]))
    return tuple(outs)
```

```python
import functools

import jax
import jax.numpy as jnp
import numpy as np
from jax import lax
from jax.experimental import pallas as pl
from jax.experimental.pallas import tpu as pltpu

HEAD_DIM = 64
N_HEADS = 8
N_KV_HEADS = 2
Q_PER_KV = N_HEADS // N_KV_HEADS
ATTN_WIDTH = N_HEADS * HEAD_DIM
KV_WIDTH = N_KV_HEADS * HEAD_DIM
N_FGROUPS = 8
FGROUP_DIM = 64
F_WIDTH = N_FGROUPS * FGROUP_DIM
GRID_W = 64
ROPE_THETA = 10000.0
EPS = 1e-6

LANES = 128
MXU_DIM = 256
ROW_SPLIT = 2
V_ROWS = 80
VMEM_LIMIT_BYTES = 56 * 1024 * 1024

TOK_TILE = 1024
Q_TILE = 1024
K_TILE = 512
Q_CHUNK = 256
ITEMS_PER_TRIP = 4
DFT_TILE = 512
REV_BLOCK = 256

BF16 = jnp.bfloat16
F32 = jnp.float32


def _params(*semantics):
    return pltpu.CompilerParams(dimension_semantics=semantics,
                                vmem_limit_bytes=VMEM_LIMIT_BYTES)


def _rope_tables(seq_len):
    rows = seq_len // GRID_W
    row = jnp.repeat(jnp.arange(rows, dtype=F32), GRID_W)
    col = jnp.tile(jnp.arange(GRID_W, dtype=F32), rows)
    n_freq = HEAD_DIM // 4
    inv_freq = ROPE_THETA ** (-jnp.arange(n_freq, dtype=F32) / n_freq)
    ang_r = row[:, None] * inv_freq[None, :]
    ang_c = col[:, None] * inv_freq[None, :]
    ang = jnp.concatenate([ang_r, ang_r, ang_c, ang_c], axis=-1)
    sign = jnp.where((jnp.arange(HEAD_DIM) % 32) < 16, -1.0, 1.0).astype(F32)
    cos = jnp.cos(ang)
    sin = jnp.sin(ang) * sign[None, :]
    reps = LANES // HEAD_DIM
    return jnp.tile(cos, (1, reps)), jnp.tile(sin, (1, reps))


def _seq_dft_tables(seq_len):
    half = seq_len // 2
    k = jnp.arange(half, dtype=jnp.int32)[:, None]
    s1 = jnp.arange(half // LANES, dtype=jnp.int32)[None, :]
    s2 = jnp.arange(LANES, dtype=jnp.int32)[None, :]
    step = 2.0 * np.pi / seq_len
    ang_hi = ((k * s1 * LANES) % seq_len).astype(F32) * step
    ang_lo = ((k * s2) % seq_len).astype(F32) * step
    ch, sh = jnp.cos(ang_hi)[:, :, None], jnp.sin(ang_hi)[:, :, None]
    cl, sl = jnp.cos(ang_lo)[:, None, :], jnp.sin(ang_lo)[:, None, :]
    col_weight = np.ones((1, half), np.float32)
    col_weight[0, 0] = 0.5
    ct = (ch * cl - sh * sl).reshape(half, half) * col_weight
    st = (sh * cl + ch * sl).reshape(half, half)
    alt = np.zeros((16, half), np.float32)
    alt[0] = (1.0 - 2.0 * (np.arange(half) % 2)) * col_weight[0]
    return ct.astype(BF16), st.astype(BF16), jnp.asarray(alt, dtype=BF16)


def _shifted_reversal(n):
    r = np.zeros((n, n), np.float32)
    t = np.arange(1, n)
    r[t, n - t] = 1.0
    return jnp.asarray(r, dtype=BF16)


def _chan_dft_table(seq_len):
    idx = np.arange(FGROUP_DIM, dtype=np.int64)
    ang = ((idx[:, None] * idx[None, :]) % FGROUP_DIM).astype(np.float64) * (2.0 * np.pi / FGROUP_DIM)
    scale = 1.0 / np.sqrt(float(seq_len) * FGROUP_DIM)
    return (np.concatenate([np.cos(ang), np.sin(ang)], axis=0) * scale).astype(np.float32)


def _wz_kernel(cs_ref, w_ref, o_ref):
    for g in range(N_FGROUPS):
        o_ref[g] = jnp.dot(cs_ref[...], w_ref[g], precision=lax.Precision.HIGHEST,
                           preferred_element_type=F32)


def _rmsnorm(x, g):
    return x * lax.rsqrt(jnp.mean(x * x, axis=-1, keepdims=True) + EPS) * g


def _proj_in_kernel(x_ref, w_ref, gn_ref, gq_ref, gk_ref, cos_ref, sin_ref, bd_ref, wz_ref,
                    q_ref, k_ref, vt_ref, ga_ref, xcs_ref, gf_ref):
    o_k = ATTN_WIDTH
    o_v = o_k + KV_WIDTH
    o_ga = o_v + KV_WIDTH
    o_f = o_ga + ATTN_WIDTH
    o_gf = o_f + F_WIDTH
    lane = lax.broadcasted_iota(jnp.int32, (1, LANES), 1)
    first_half = (lane % 32) < 16
    scale = HEAD_DIM ** -0.5 * np.log2(np.e)
    sub = x_ref.shape[1] // ROW_SPLIT

    def head_mean_sq(zz, width):
        sq = (zz * zz).astype(BF16)
        step = min(width, MXU_DIM)
        return jnp.concatenate(
            [jnp.dot(sq[:, t:t + step], bd_ref[t:t + step, t:t + step],
                     preferred_element_type=F32) for t in range(0, width, step)], axis=1)

    for r in range(ROW_SPLIT):
        rows = slice(r * sub, (r + 1) * sub)
        cos = cos_ref[rows, :]
        sin = sin_ref[rows, :]
        u = _rmsnorm(x_ref[0, rows, :], gn_ref[...]).astype(BF16)

        def project(lo, hi):
            return jnp.dot(u, w_ref[:, lo:hi], preferred_element_type=F32)

        def norm_rope(zz, msq, g):
            n = zz * lax.rsqrt(msq + EPS) * g
            outs = []
            for c in range(zz.shape[1] // LANES):
                xc = n[:, c * LANES:(c + 1) * LANES]
                rot = jnp.where(first_half, pltpu.roll(xc, LANES - 16, 1), pltpu.roll(xc, 16, 1))
                outs.append(xc * cos + rot * sin)
            return outs

        z_qkv = project(0, o_ga)
        z_ga = project(o_ga, o_f)
        msq_q = head_mean_sq(z_qkv[:, 0:o_k], ATTN_WIDTH)
        msq_k = head_mean_sq(z_qkv[:, o_k:o_v], KV_WIDTH)
        z_f = project(o_f, o_gf)

        vt = z_qkv[:, o_v:o_ga].T.astype(BF16)
        ones = jnp.ones((V_ROWS - HEAD_DIM, sub), BF16)
        for g in range(N_KV_HEADS):
            vt_ref[0, g, 0:HEAD_DIM, rows] = vt[g * HEAD_DIM:(g + 1) * HEAD_DIM]
            vt_ref[0, g, HEAD_DIM:V_ROWS, rows] = ones
        ga_ref[0, rows, :] = (z_ga * jax.nn.sigmoid(z_ga)).astype(BF16)

        z_gf = project(o_gf, o_gf + F_WIDTH)
        f = z_f.astype(BF16)
        for t in range(0, F_WIDTH, MXU_DIM):
            for half in (0, F_WIDTH):
                xcs_ref[0, rows, half + t:half + t + MXU_DIM] = jnp.dot(
                    f[:, t:t + MXU_DIM], wz_ref[t:t + MXU_DIM, half + t:half + t + MXU_DIM],
                    preferred_element_type=F32).astype(BF16)

        for c, qc in enumerate(norm_rope(z_qkv[:, 0:o_k], msq_q, gq_ref[...])):
            qt = (qc * scale).T.astype(BF16)
            q_ref[0, 2 * c, :, rows] = qt[0:HEAD_DIM]
            q_ref[0, 2 * c + 1, :, rows] = qt[HEAD_DIM:LANES]
        (kc,) = norm_rope(z_qkv[:, o_k:o_v], msq_k, gk_ref[...])
        kc = kc.astype(BF16)
        k_ref[0, 0, rows, :] = kc[:, 0:HEAD_DIM]
        k_ref[0, 1, rows, :] = kc[:, HEAD_DIM:LANES]
        gf_ref[0, rows, :] = (z_gf * jax.nn.sigmoid(z_gf)).astype(BF16)


def _attn_kernel(q_ref, k_ref, vt_ref, o_ref, m_ref, acc_ref, s0_ref, s1_ref, t0_ref, t1_ref, *,
                 q_tile, k_tile, q_chunk):
    seq = k_ref.shape[2]
    block_cols = Q_PER_KV * q_tile
    n_chunks = block_cols // q_chunk
    n_tiles = seq // k_tile
    n_items = (seq // q_tile) * n_tiles
    assert n_tiles % 2 == 0
    m_ref[...] = jnp.full(m_ref.shape, -jnp.inf, F32)
    acc_ref[...] = jnp.zeros(acc_ref.shape, F32)

    buffers = ((s0_ref, t0_ref), (s1_ref, t1_ref))

    def scores(item, c, buf):
        s_ref, t_ref = buf
        h, r = divmod(c * q_chunk, q_tile)
        keys = pl.ds(pl.multiple_of((item % n_tiles) * k_tile, k_tile), k_tile)
        queries = pl.ds(pl.multiple_of((item // n_tiles) * q_tile + r, q_chunk), q_chunk)
        s = jnp.dot(k_ref[0, 0, keys, :], q_ref[0, h, :, queries],
                    preferred_element_type=F32)
        s_ref[c] = s
        t_ref[c] = jnp.max(s, axis=0, keepdims=True)

    def softmax_pv(item, c, buf):
        s_ref, t_ref = buf
        keys = pl.ds(pl.multiple_of((item % n_tiles) * k_tile, k_tile), k_tile)
        cols = pl.ds(pl.multiple_of((item // n_tiles) * block_cols + c * q_chunk, q_chunk),
                     q_chunk)
        m_old = m_ref[:, cols]
        m_new = jnp.maximum(m_old, t_ref[c])
        alpha = jnp.exp2(m_old - m_new)
        p = jnp.exp2(s_ref[c] - m_new).astype(BF16)
        acc_ref[:, cols] = alpha * acc_ref[:, cols] + jnp.dot(
            vt_ref[0, 0, :, keys], p, preferred_element_type=F32)
        m_ref[:, cols] = m_new

    def step(item, parity, prefetch=True):
        for c in range(n_chunks):
            if prefetch:
                scores(item + 1, c, buffers[1 - parity])
            softmax_pv(item, c, buffers[parity])

    for c in range(n_chunks):
        scores(0, c, buffers[0])

    def body(i, carry):
        for t in range(ITEMS_PER_TRIP):
            step(ITEMS_PER_TRIP * i + t, t % 2)
        return carry

    assert n_items % ITEMS_PER_TRIP == 0 and ITEMS_PER_TRIP % 2 == 0
    lax.fori_loop(0, n_items // ITEMS_PER_TRIP - 1, body, 0)
    for t in range(n_items - ITEMS_PER_TRIP, n_items):
        step(t, t % 2, prefetch=t + 1 < n_items)

    pad = jnp.zeros((LANES - HEAD_DIM, q_tile), F32)
    for blk in range(seq // q_tile):
        for h in range(Q_PER_KV):
            col0 = blk * block_cols + h * q_tile
            acc = acc_ref[:, col0:col0 + q_tile]
            o = acc[0:HEAD_DIM] / acc[HEAD_DIM:HEAD_DIM + 1]
            ot = jnp.concatenate([o, pad], axis=0).T
            o_ref[0, blk * q_tile:(blk + 1) * q_tile, h * HEAD_DIM:(h + 1) * HEAD_DIM] = (
                ot[:, 0:HEAD_DIM].astype(BF16))


def _dft_kernel(ct_ref, st_ref, alt_ref, rev_ref, x_ref, y_ref, pq_ref, u_ref):
    seq = x_ref.shape[1]
    half = seq // 2
    nb = half // REV_BLOCK
    rev = rev_ref[...]

    def is_row0(width):
        return lax.broadcasted_iota(jnp.int32, (REV_BLOCK, width), 0) == 0

    for i in range(nb):
        lo = slice(i * REV_BLOCK, (i + 1) * REV_BLOCK)
        mirror = jnp.dot(rev, x_ref[0, (2 * nb - 1 - i) * REV_BLOCK:(2 * nb - i) * REV_BLOCK, :],
                         preferred_element_type=F32)
        first = ((2 * nb - i) * REV_BLOCK) % seq
        mirror = jnp.where(is_row0(2 * F_WIDTH), x_ref[0, first:first + 1, :].astype(F32), mirror)
        xb = x_ref[0, lo, :].astype(F32)
        pq_ref[lo, 0:F_WIDTH] = (xb[:, 0:F_WIDTH] + mirror[:, 0:F_WIDTH]).astype(BF16)
        pq_ref[lo, F_WIDTH:] = (xb[:, F_WIDTH:] - mirror[:, F_WIDTH:]).astype(BF16)

    p_mid = x_ref[0, half:half + 1, 0:F_WIDTH].astype(F32)
    odd_k = (lax.broadcasted_iota(jnp.int32, (DFT_TILE, F_WIDTH), 0) & 1) == 1
    p_mid_signed = jnp.where(odd_k, -p_mid, p_mid)
    for kb in range(half // DFT_TILE):
        rows = slice(kb * DFT_TILE, (kb + 1) * DFT_TILE)
        a = jnp.dot(ct_ref[rows, :], pq_ref[:, 0:F_WIDTH], preferred_element_type=F32)
        a = a + p_mid_signed
        b = jnp.dot(st_ref[rows, :], pq_ref[:, F_WIDTH:], preferred_element_type=F32)
        y_ref[0, rows, :] = (a - b).astype(BF16)
        u_ref[rows, :] = (a + b).astype(BF16)

    y_mid = jnp.dot(alt_ref[...], pq_ref[:, 0:F_WIDTH], preferred_element_type=F32)[0:1] + p_mid

    for i in range(nb):
        mirror = jnp.dot(rev, u_ref[(nb - 1 - i) * REV_BLOCK:(nb - i) * REV_BLOCK, :],
                         preferred_element_type=F32)
        if i == 0:
            first_row = y_mid
        else:
            first_row = u_ref[(nb - i) * REV_BLOCK:(nb - i) * REV_BLOCK + 1, :].astype(F32)
        y_ref[0, half + i * REV_BLOCK:half + (i + 1) * REV_BLOCK, :] = jnp.where(
            is_row0(F_WIDTH), first_row, mirror).astype(BF16)


def _proj_out_kernel(x_ref, a_ref, ga_ref, fm_ref, gf_ref, p_ref, wo_ref, wg_ref, wp_ref,
                     gple_ref, gfin_ref, y_ref, *, final_norm):
    sub = x_ref.shape[1] // ROW_SPLIT
    halves = [slice(r * sub, (r + 1) * sub) for r in range(ROW_SPLIT)]
    hs, ples = [], []
    for rows in halves:
        mix_a = (a_ref[0, rows, :].astype(F32) * ga_ref[0, rows, :].astype(F32)).astype(BF16)
        mix_f = (fm_ref[0, rows, :].astype(F32) * gf_ref[0, rows, :].astype(F32)).astype(BF16)
        h = x_ref[0, rows, :] + jnp.dot(mix_a, wo_ref[0:ATTN_WIDTH, :],
                                        preferred_element_type=F32)
        hs.append(h + jnp.dot(mix_f, wo_ref[ATTN_WIDTH:ATTN_WIDTH + F_WIDTH, :],
                              preferred_element_type=F32))
        ples.append(jnp.dot(p_ref[0, rows, :].astype(BF16), wp_ref[...],
                            preferred_element_type=F32))
    piece = sub // 2
    for rows, h_half, ple_half in zip(halves, hs, ples):
        for lo in range(0, sub, piece):
            h = h_half[lo:lo + piece]
            hn = _rmsnorm(h, gple_ref[...]).astype(BF16)
            gate = jax.nn.sigmoid(jnp.dot(hn, wg_ref[...], preferred_element_type=F32))
            h = h + ple_half[lo:lo + piece] * gate
            if final_norm:
                h = _rmsnorm(h, gfin_ref[...])
            y_ref[0, rows.start + lo:rows.start + lo + piece, :] = h


def _const_spec(shape):
    return pl.BlockSpec(shape, lambda *_: (0,) * len(shape))


def _fourier_weights(w_fmix, seq_len):
    cs = jnp.asarray(_chan_dft_table(seq_len))
    out = pl.pallas_call(
        _wz_kernel,
        out_shape=jax.ShapeDtypeStruct((N_FGROUPS, 2 * FGROUP_DIM, FGROUP_DIM), F32),
        name="fourier_weights",
    )(cs, w_fmix)
    eye = jnp.eye(N_FGROUPS, dtype=F32)

    def block_diag(w):
        return (eye[:, None, :, None] * w[:, :, None, :]).reshape(F_WIDTH, F_WIDTH)

    return jnp.concatenate([block_diag(out[:, :FGROUP_DIM]), block_diag(out[:, FGROUP_DIM:])],
                           axis=1).astype(BF16)


def _layer(x, p, w, *, final_norm):
    batch, seq, d_model = x.shape
    in_width = w["w_in"].shape[1]
    nt = seq // TOK_TILE
    tok = lambda width: pl.BlockSpec((1, TOK_TILE, width), lambda b, t: (b, t, 0))

    q, k, vt, ga, xcs, gf = pl.pallas_call(
        _proj_in_kernel,
        grid=(batch, nt),
        in_specs=[
            tok(d_model),
            _const_spec((d_model, in_width)),
            _const_spec((1, d_model)),
            _const_spec((1, ATTN_WIDTH)),
            _const_spec((1, KV_WIDTH)),
            pl.BlockSpec((TOK_TILE, LANES), lambda b, t: (t, 0)),
            pl.BlockSpec((TOK_TILE, LANES), lambda b, t: (t, 0)),
            _const_spec((ATTN_WIDTH, ATTN_WIDTH)),
            _const_spec((F_WIDTH, 2 * F_WIDTH)),
        ],
        out_specs=[
            pl.BlockSpec((1, N_HEADS, HEAD_DIM, TOK_TILE), lambda b, t: (b, 0, 0, t)),
            pl.BlockSpec((1, N_KV_HEADS, TOK_TILE, HEAD_DIM), lambda b, t: (b, 0, t, 0)),
            pl.BlockSpec((1, N_KV_HEADS, V_ROWS, TOK_TILE), lambda b, t: (b, 0, 0, t)),
            tok(ATTN_WIDTH),
            tok(2 * F_WIDTH),
            tok(F_WIDTH),
        ],
        out_shape=[
            jax.ShapeDtypeStruct((batch, N_HEADS, HEAD_DIM, seq), BF16),
            jax.ShapeDtypeStruct((batch, N_KV_HEADS, seq, HEAD_DIM), BF16),
            jax.ShapeDtypeStruct((batch, N_KV_HEADS, V_ROWS, seq), BF16),
            jax.ShapeDtypeStruct((batch, seq, ATTN_WIDTH), BF16),
            jax.ShapeDtypeStruct((batch, seq, 2 * F_WIDTH), BF16),
            jax.ShapeDtypeStruct((batch, seq, F_WIDTH), BF16),
        ],
        compiler_params=_params("parallel", "parallel"),
        name="proj_in",
    )(x, w["w_in"], w["g_norm"], w["g_q"], w["g_k"], w["cos"], w["sin"], w["bd"], w["wz"])

    kv_width = Q_PER_KV * HEAD_DIM
    a = pl.pallas_call(
        functools.partial(_attn_kernel, q_tile=Q_TILE, k_tile=K_TILE, q_chunk=Q_CHUNK),
        grid=(batch, N_KV_HEADS),
        in_specs=[
            pl.BlockSpec((1, Q_PER_KV, HEAD_DIM, seq), lambda b, g: (b, g, 0, 0)),
            pl.BlockSpec((1, 1, seq, HEAD_DIM), lambda b, g: (b, g, 0, 0)),
            pl.BlockSpec((1, 1, V_ROWS, seq), lambda b, g: (b, g, 0, 0)),
        ],
        out_specs=pl.BlockSpec((1, seq, kv_width), lambda b, g: (b, 0, g)),
        out_shape=jax.ShapeDtypeStruct((batch, seq, ATTN_WIDTH), BF16),
        scratch_shapes=[pltpu.VMEM((1, Q_PER_KV * seq), F32),
                        pltpu.VMEM((V_ROWS, Q_PER_KV * seq), F32),
                        pltpu.VMEM((Q_PER_KV * Q_TILE // Q_CHUNK, K_TILE, Q_CHUNK), F32),
                        pltpu.VMEM((Q_PER_KV * Q_TILE // Q_CHUNK, K_TILE, Q_CHUNK), F32),
                        pltpu.VMEM((Q_PER_KV * Q_TILE // Q_CHUNK, 1, Q_CHUNK), F32),
                        pltpu.VMEM((Q_PER_KV * Q_TILE // Q_CHUNK, 1, Q_CHUNK), F32)],
        compiler_params=_params("parallel", "parallel"),
        name="attn",
    )(q, k, vt)

    half = seq // 2
    resident = lambda shape: pl.BlockSpec(shape, lambda b: (0,) * len(shape),
                                          pipeline_mode=pl.Buffered(1))
    fm = pl.pallas_call(
        _dft_kernel,
        grid=(batch,),
        in_specs=[
            resident((half, half)),
            resident((half, half)),
            resident(w["alt"].shape),
            resident((REV_BLOCK, REV_BLOCK)),
            pl.BlockSpec((1, seq, 2 * F_WIDTH), lambda b: (b, 0, 0)),
        ],
        out_specs=pl.BlockSpec((1, seq, F_WIDTH), lambda b: (b, 0, 0)),
        out_shape=jax.ShapeDtypeStruct((batch, seq, F_WIDTH), BF16),
        scratch_shapes=[pltpu.VMEM((half, 2 * F_WIDTH), BF16),
                        pltpu.VMEM((half, F_WIDTH), BF16)],
        compiler_params=_params("parallel"),
        name="dft",
    )(w["ct"], w["st"], w["alt"], w["rev"], xcs)

    ple_dim = p.shape[-1]
    return pl.pallas_call(
        functools.partial(_proj_out_kernel, final_norm=final_norm),
        grid=(batch, nt),
        in_specs=[
            tok(d_model), tok(ATTN_WIDTH), tok(ATTN_WIDTH), tok(F_WIDTH), tok(F_WIDTH),
            tok(ple_dim),
            _const_spec((ATTN_WIDTH + F_WIDTH, d_model)),
            _const_spec((d_model, d_model)),
            _const_spec((ple_dim, d_model)),
            _const_spec((1, d_model)),
            _const_spec((1, d_model)),
        ],
        out_specs=tok(d_model),
        out_shape=jax.ShapeDtypeStruct((batch, seq, d_model), F32),
        compiler_params=_params("parallel", "parallel"),
        name="proj_out",
    )(x, a, ga, fm, gf, p, w["w_out"], w["w_ple_gate"], w["w_ple"], w["g_ple"], w["g_final"])


def _layer_weights(i, seq, g_norm, w_in, g_q, g_k, w_fmix, w_out, g_ple, w_ple_gate, w_ple, g_final):
    cos, sin = _rope_tables(seq)
    ct, st, alt = _seq_dft_tables(seq)
    heads = np.arange(ATTN_WIDTH) // HEAD_DIM
    bd = jnp.asarray((heads[:, None] == heads[None, :]).astype(np.float32) / HEAD_DIM, dtype=BF16)
    return {
        "w_in": w_in[i].astype(BF16),
        "g_norm": g_norm[i][None, :],
        "g_q": jnp.tile(g_q[i], N_HEADS)[None, :],
        "g_k": jnp.tile(g_k[i], N_KV_HEADS)[None, :],
        "cos": cos, "sin": sin, "bd": bd,
        "wz": _fourier_weights(w_fmix[i], seq),
        "ct": ct, "st": st, "alt": alt, "rev": _shifted_reversal(REV_BLOCK),
        "w_out": w_out[i].astype(BF16),
        "w_ple_gate": w_ple_gate[i].astype(BF16),
        "w_ple": w_ple[i].astype(BF16),
        "g_ple": g_ple[i][None, :],
        "g_final": g_final[None, :],
    }


def _trunk(x, p, layer_weights):
    h = x
    depth = len(layer_weights)
    for i, w in enumerate(layer_weights):
        h = _layer(h, p[i], w, final_norm=(i == depth - 1))
    return h


def kernel(x_prompt, x_sample, p_prompt, p_sample, g_norm, w_in, g_q, g_k, w_fmix, w_out,
           g_ple, w_ple_gate, w_ple, g_final):
    depth = g_norm.shape[0]
    outs = []
    weights_by_seq = {}
    for x, p in ((x_prompt, p_prompt), (x_sample, p_sample)):
        seq = x.shape[1]
        if seq not in weights_by_seq:
            weights_by_seq[seq] = [
                _layer_weights(i, seq, g_norm, w_in, g_q, g_k, w_fmix, w_out, g_ple,
                               w_ple_gate, w_ple, g_final) for i in range(depth)]
        outs.append(_trunk(x, p, weights_by_seq[seq]))
    return tuple(outs)
```

```python
import functools

import jax
import jax.numpy as jnp
import numpy as np
from jax import lax
from jax.experimental import pallas as pl
from jax.experimental.pallas import tpu as pltpu

HEAD_DIM = 64
N_HEADS = 8
N_KV_HEADS = 2
Q_PER_KV = N_HEADS // N_KV_HEADS
ATTN_WIDTH = N_HEADS * HEAD_DIM
KV_WIDTH = N_KV_HEADS * HEAD_DIM
N_FGROUPS = 8
FGROUP_DIM = 64
F_WIDTH = N_FGROUPS * FGROUP_DIM
GRID_W = 64
ROPE_THETA = 10000.0
EPS = 1e-6

LANES = 128
MXU_DIM = 256
ROW_SPLIT = 2
V_ROWS = 80
VMEM_LIMIT_BYTES = 56 * 1024 * 1024

TOK_TILE = 1024
Q_TILE = 1024
K_TILE = 512
Q_CHUNK = 256
ITEMS_PER_TRIP = 8
DFT_TILE = 512
REV_BLOCK = 256

BF16 = jnp.bfloat16
F32 = jnp.float32


def _params(*semantics):
    return pltpu.CompilerParams(dimension_semantics=semantics,
                                vmem_limit_bytes=VMEM_LIMIT_BYTES)


def _rope_tables(seq_len):
    rows = seq_len // GRID_W
    row = jnp.repeat(jnp.arange(rows, dtype=F32), GRID_W)
    col = jnp.tile(jnp.arange(GRID_W, dtype=F32), rows)
    n_freq = HEAD_DIM // 4
    inv_freq = ROPE_THETA ** (-jnp.arange(n_freq, dtype=F32) / n_freq)
    ang_r = row[:, None] * inv_freq[None, :]
    ang_c = col[:, None] * inv_freq[None, :]
    ang = jnp.concatenate([ang_r, ang_r, ang_c, ang_c], axis=-1)
    sign = jnp.where((jnp.arange(HEAD_DIM) % 32) < 16, -1.0, 1.0).astype(F32)
    cos = jnp.cos(ang)
    sin = jnp.sin(ang) * sign[None, :]
    reps = LANES // HEAD_DIM
    return jnp.tile(cos, (1, reps)), jnp.tile(sin, (1, reps))


def _seq_dft_tables(seq_len):
    half = seq_len // 2
    s = jnp.arange(half, dtype=jnp.int32)[None, :]
    k1 = jnp.arange(half // GRID_W, dtype=jnp.int32)[:, None]
    k2 = jnp.arange(GRID_W, dtype=jnp.int32)[:, None]
    step = 2.0 * np.pi / seq_len
    ang_hi = ((k1 * GRID_W * s) % seq_len).astype(F32) * step
    ang_lo = ((k2 * s) % seq_len).astype(F32) * step
    ch, sh = jnp.cos(ang_hi)[:, None, :], jnp.sin(ang_hi)[:, None, :]
    cl, sl = jnp.cos(ang_lo)[None, :, :], jnp.sin(ang_lo)[None, :, :]
    col_weight = np.ones((1, half), np.float32)
    col_weight[0, 0] = 0.5
    ct = (ch * cl - sh * sl).reshape(half, half) * col_weight
    st = (sh * cl + ch * sl).reshape(half, half)
    alt = np.zeros((16, half), np.float32)
    alt[0] = (1.0 - 2.0 * (np.arange(half) % 2)) * col_weight[0]
    return ct.astype(BF16), st.astype(BF16), jnp.asarray(alt, dtype=BF16)


def _shifted_reversal(n):
    r = np.zeros((n, n), np.float32)
    t = np.arange(1, n)
    r[t, n - t] = 1.0
    return jnp.asarray(r, dtype=BF16)


def _chan_dft_table(seq_len):
    idx = np.arange(FGROUP_DIM, dtype=np.int64)
    ang = ((idx[:, None] * idx[None, :]) % FGROUP_DIM).astype(np.float64) * (2.0 * np.pi / FGROUP_DIM)
    scale = 1.0 / np.sqrt(float(seq_len) * FGROUP_DIM)
    return (np.concatenate([np.cos(ang), np.sin(ang)], axis=0) * scale).astype(np.float32)


def _wz_kernel(cs_ref, w_ref, o_ref):
    for g in range(N_FGROUPS):
        o_ref[g] = jnp.dot(cs_ref[...], w_ref[g], precision=lax.Precision.HIGHEST,
                           preferred_element_type=F32)


def _rmsnorm(x, g):
    return x * lax.rsqrt(jnp.mean(x * x, axis=-1, keepdims=True) + EPS) * g


def _proj_in_kernel(x_ref, w_ref, gn_ref, gq_ref, gk_ref, cos_ref, sin_ref, bd_ref, wz_ref,
                    q_ref, k_ref, vt_ref, ga_ref, xcs_ref, gf_ref):
    o_k = ATTN_WIDTH
    o_v = o_k + KV_WIDTH
    o_ga = o_v + KV_WIDTH
    o_f = o_ga + ATTN_WIDTH
    o_gf = o_f + F_WIDTH
    lane = lax.broadcasted_iota(jnp.int32, (1, LANES), 1)
    first_half = (lane % 32) < 16
    scale = HEAD_DIM ** -0.5 * np.log2(np.e)
    sub = x_ref.shape[1] // ROW_SPLIT

    def head_mean_sq(zz, width):
        sq = (zz * zz).astype(BF16)
        step = min(width, MXU_DIM)
        return jnp.concatenate(
            [jnp.dot(sq[:, t:t + step], bd_ref[t:t + step, t:t + step],
                     preferred_element_type=F32) for t in range(0, width, step)], axis=1)

    for r in range(ROW_SPLIT):
        rows = slice(r * sub, (r + 1) * sub)
        cos = cos_ref[rows, :]
        sin = sin_ref[rows, :]
        u = _rmsnorm(x_ref[0, rows, :], gn_ref[...]).astype(BF16)

        def project(lo, hi):
            return jnp.dot(u, w_ref[:, lo:hi], preferred_element_type=F32)

        def norm_rope(zz, msq, g):
            n = zz * lax.rsqrt(msq + EPS) * g
            outs = []
            for c in range(zz.shape[1] // LANES):
                xc = n[:, c * LANES:(c + 1) * LANES]
                rot = jnp.where(first_half, pltpu.roll(xc, LANES - 16, 1), pltpu.roll(xc, 16, 1))
                outs.append(xc * cos + rot * sin)
            return outs

        z_qkv = project(0, o_ga)
        z_ga = project(o_ga, o_f)
        msq_q = head_mean_sq(z_qkv[:, 0:o_k], ATTN_WIDTH)
        msq_k = head_mean_sq(z_qkv[:, o_k:o_v], KV_WIDTH)
        z_f = project(o_f, o_gf)

        vt = z_qkv[:, o_v:o_ga].T.astype(BF16)
        ones = jnp.ones((V_ROWS - HEAD_DIM, sub), BF16)
        for g in range(N_KV_HEADS):
            vt_ref[0, g, 0:HEAD_DIM, rows] = vt[g * HEAD_DIM:(g + 1) * HEAD_DIM]
            vt_ref[0, g, HEAD_DIM:V_ROWS, rows] = ones
        ga_ref[0, rows, :] = (z_ga * jax.nn.sigmoid(z_ga)).astype(BF16)

        z_gf = project(o_gf, o_gf + F_WIDTH)
        f = z_f.astype(BF16)
        for t in range(0, F_WIDTH, MXU_DIM):
            for half in (0, F_WIDTH):
                xcs_ref[0, rows, half + t:half + t + MXU_DIM] = jnp.dot(
                    f[:, t:t + MXU_DIM], wz_ref[t:t + MXU_DIM, half + t:half + t + MXU_DIM],
                    preferred_element_type=F32).astype(BF16)

        for c, qc in enumerate(norm_rope(z_qkv[:, 0:o_k], msq_q, gq_ref[...])):
            qt = (qc * scale).T.astype(BF16)
            q_ref[0, 2 * c, :, rows] = qt[0:HEAD_DIM]
            q_ref[0, 2 * c + 1, :, rows] = qt[HEAD_DIM:LANES]
        (kc,) = norm_rope(z_qkv[:, o_k:o_v], msq_k, gk_ref[...])
        kc = kc.astype(BF16)
        k_ref[0, 0, rows, :] = kc[:, 0:HEAD_DIM]
        k_ref[0, 1, rows, :] = kc[:, HEAD_DIM:LANES]
        gf_ref[0, rows, :] = (z_gf * jax.nn.sigmoid(z_gf)).astype(BF16)


def _attn_kernel(q_ref, k_ref, vt_ref, o_ref, m_ref, acc_ref, s0_ref, s1_ref, t0_ref, t1_ref, *,
                 q_tile, k_tile, q_chunk):
    seq = k_ref.shape[2]
    block_cols = Q_PER_KV * q_tile
    n_chunks = block_cols // q_chunk
    n_tiles = seq // k_tile
    n_items = (seq // q_tile) * n_tiles
    assert n_tiles % 2 == 0
    m_ref[...] = jnp.full(m_ref.shape, -jnp.inf, F32)
    acc_ref[...] = jnp.zeros(acc_ref.shape, F32)

    buffers = ((s0_ref, t0_ref), (s1_ref, t1_ref))

    def scores(item, c, buf):
        s_ref, t_ref = buf
        h, r = divmod(c * q_chunk, q_tile)
        keys = pl.ds(pl.multiple_of((item % n_tiles) * k_tile, k_tile), k_tile)
        queries = pl.ds(pl.multiple_of((item // n_tiles) * q_tile + r, q_chunk), q_chunk)
        s = jnp.dot(k_ref[0, 0, keys, :], q_ref[0, h, :, queries],
                    preferred_element_type=F32)
        s_ref[c] = s
        t_ref[c] = jnp.max(s, axis=0, keepdims=True)

    def softmax_pv(item, c, buf):
        s_ref, t_ref = buf
        keys = pl.ds(pl.multiple_of((item % n_tiles) * k_tile, k_tile), k_tile)
        cols = pl.ds(pl.multiple_of((item // n_tiles) * block_cols + c * q_chunk, q_chunk),
                     q_chunk)
        m_old = m_ref[:, cols]
        m_new = jnp.maximum(m_old, t_ref[c])
        alpha = jnp.exp2(m_old - m_new)
        p = jnp.exp2(s_ref[c] - m_new).astype(BF16)
        acc_ref[:, cols] = alpha * acc_ref[:, cols] + jnp.dot(
            vt_ref[0, 0, :, keys], p, preferred_element_type=F32)
        m_ref[:, cols] = m_new

    def step(item, parity, prefetch=True):
        for c in range(n_chunks):
            if prefetch:
                scores(item + 1, c, buffers[1 - parity])
            softmax_pv(item, c, buffers[parity])

    for c in range(n_chunks):
        scores(0, c, buffers[0])

    def body(i, carry):
        for t in range(ITEMS_PER_TRIP):
            step(ITEMS_PER_TRIP * i + t, t % 2)
        return carry

    assert n_items % ITEMS_PER_TRIP == 0 and ITEMS_PER_TRIP % 2 == 0
    lax.fori_loop(0, n_items // ITEMS_PER_TRIP - 1, body, 0)
    for t in range(n_items - ITEMS_PER_TRIP, n_items):
        step(t, t % 2, prefetch=t + 1 < n_items)

    pad = jnp.zeros((LANES - HEAD_DIM, q_tile), F32)
    for blk in range(seq // q_tile):
        for h in range(Q_PER_KV):
            col0 = blk * block_cols + h * q_tile
            acc = acc_ref[:, col0:col0 + q_tile]
            o = acc[0:HEAD_DIM] / acc[HEAD_DIM:HEAD_DIM + 1]
            ot = jnp.concatenate([o, pad], axis=0).T
            o_ref[0, blk * q_tile:(blk + 1) * q_tile, h * HEAD_DIM:(h + 1) * HEAD_DIM] = (
                ot[:, 0:HEAD_DIM].astype(BF16))


def _dft_kernel(ct_ref, st_ref, alt_ref, rev_ref, x_ref, y_ref, pq_ref, u_ref):
    seq = x_ref.shape[1]
    half = seq // 2
    nb = half // REV_BLOCK
    rev = rev_ref[...]

    def is_row0(width):
        return lax.broadcasted_iota(jnp.int32, (REV_BLOCK, width), 0) == 0

    for i in range(nb):
        lo = slice(i * REV_BLOCK, (i + 1) * REV_BLOCK)
        mirror = jnp.dot(rev, x_ref[0, (2 * nb - 1 - i) * REV_BLOCK:(2 * nb - i) * REV_BLOCK, :],
                         preferred_element_type=F32)
        first = ((2 * nb - i) * REV_BLOCK) % seq
        mirror = jnp.where(is_row0(2 * F_WIDTH), x_ref[0, first:first + 1, :].astype(F32), mirror)
        xb = x_ref[0, lo, :].astype(F32)
        pq_ref[lo, 0:F_WIDTH] = (xb[:, 0:F_WIDTH] + mirror[:, 0:F_WIDTH]).astype(BF16)
        pq_ref[lo, F_WIDTH:] = (xb[:, F_WIDTH:] - mirror[:, F_WIDTH:]).astype(BF16)

    p_mid = x_ref[0, half:half + 1, 0:F_WIDTH].astype(F32)
    odd_k = (lax.broadcasted_iota(jnp.int32, (DFT_TILE, F_WIDTH), 0) & 1) == 1
    p_mid_signed = jnp.where(odd_k, -p_mid, p_mid)
    for kb in range(half // DFT_TILE):
        rows = slice(kb * DFT_TILE, (kb + 1) * DFT_TILE)
        a = jnp.dot(ct_ref[rows, :], pq_ref[:, 0:F_WIDTH], preferred_element_type=F32)
        a = a + p_mid_signed
        b = jnp.dot(st_ref[rows, :], pq_ref[:, F_WIDTH:], preferred_element_type=F32)
        y_ref[0, rows, :] = (a - b).astype(BF16)
        u_ref[rows, :] = (a + b).astype(BF16)

    y_mid = jnp.dot(alt_ref[...], pq_ref[:, 0:F_WIDTH], preferred_element_type=F32)[0:1] + p_mid

    for i in range(nb):
        mirror = jnp.dot(rev, u_ref[(nb - 1 - i) * REV_BLOCK:(nb - i) * REV_BLOCK, :],
                         preferred_element_type=F32)
        if i == 0:
            first_row = y_mid
        else:
            first_row = u_ref[(nb - i) * REV_BLOCK:(nb - i) * REV_BLOCK + 1, :].astype(F32)
        y_ref[0, half + i * REV_BLOCK:half + (i + 1) * REV_BLOCK, :] = jnp.where(
            is_row0(F_WIDTH), first_row, mirror).astype(BF16)


def _proj_out_kernel(x_ref, a_ref, ga_ref, fm_ref, gf_ref, p_ref, wo_ref, wg_ref, wp_ref,
                     gple_ref, gfin_ref, y_ref, *, final_norm):
    sub = x_ref.shape[1] // ROW_SPLIT
    halves = [slice(r * sub, (r + 1) * sub) for r in range(ROW_SPLIT)]
    hs, ples = [], []
    for rows in halves:
        mix_a = (a_ref[0, rows, :].astype(F32) * ga_ref[0, rows, :].astype(F32)).astype(BF16)
        mix_f = (fm_ref[0, rows, :].astype(F32) * gf_ref[0, rows, :].astype(F32)).astype(BF16)
        h = x_ref[0, rows, :] + jnp.dot(mix_a, wo_ref[0:ATTN_WIDTH, :],
                                        preferred_element_type=F32)
        hs.append(h + jnp.dot(mix_f, wo_ref[ATTN_WIDTH:ATTN_WIDTH + F_WIDTH, :],
                              preferred_element_type=F32))
        ples.append(jnp.dot(p_ref[0, rows, :].astype(BF16), wp_ref[...],
                            preferred_element_type=F32))
    piece = sub // 2
    for rows, h_half, ple_half in zip(halves, hs, ples):
        for lo in range(0, sub, piece):
            h = h_half[lo:lo + piece]
            hn = _rmsnorm(h, gple_ref[...]).astype(BF16)
            gate = jax.nn.sigmoid(jnp.dot(hn, wg_ref[...], preferred_element_type=F32))
            h = h + ple_half[lo:lo + piece] * gate
            if final_norm:
                h = _rmsnorm(h, gfin_ref[...])
            y_ref[0, rows.start + lo:rows.start + lo + piece, :] = h


def _const_spec(shape):
    return pl.BlockSpec(shape, lambda *_: (0,) * len(shape))


def _fourier_weights(w_fmix, seq_len):
    cs = jnp.asarray(_chan_dft_table(seq_len))
    out = pl.pallas_call(
        _wz_kernel,
        out_shape=jax.ShapeDtypeStruct((N_FGROUPS, 2 * FGROUP_DIM, FGROUP_DIM), F32),
        name="fourier_weights",
    )(cs, w_fmix)
    eye = jnp.eye(N_FGROUPS, dtype=F32)

    def block_diag(w):
        return (eye[:, None, :, None] * w[:, :, None, :]).reshape(F_WIDTH, F_WIDTH)

    return jnp.concatenate([block_diag(out[:, :FGROUP_DIM]), block_diag(out[:, FGROUP_DIM:])],
                           axis=1).astype(BF16)


def _layer(x, p, w, *, final_norm):
    batch, seq, d_model = x.shape
    in_width = w["w_in"].shape[1]
    nt = seq // TOK_TILE
    tok = lambda width: pl.BlockSpec((1, TOK_TILE, width), lambda b, t: (b, t, 0))

    q, k, vt, ga, xcs, gf = pl.pallas_call(
        _proj_in_kernel,
        grid=(batch, nt),
        in_specs=[
            tok(d_model),
            _const_spec((d_model, in_width)),
            _const_spec((1, d_model)),
            _const_spec((1, ATTN_WIDTH)),
            _const_spec((1, KV_WIDTH)),
            pl.BlockSpec((TOK_TILE, LANES), lambda b, t: (t, 0)),
            pl.BlockSpec((TOK_TILE, LANES), lambda b, t: (t, 0)),
            _const_spec((ATTN_WIDTH, ATTN_WIDTH)),
            _const_spec((F_WIDTH, 2 * F_WIDTH)),
        ],
        out_specs=[
            pl.BlockSpec((1, N_HEADS, HEAD_DIM, TOK_TILE), lambda b, t: (b, 0, 0, t)),
            pl.BlockSpec((1, N_KV_HEADS, TOK_TILE, HEAD_DIM), lambda b, t: (b, 0, t, 0)),
            pl.BlockSpec((1, N_KV_HEADS, V_ROWS, TOK_TILE), lambda b, t: (b, 0, 0, t)),
            tok(ATTN_WIDTH),
            tok(2 * F_WIDTH),
            tok(F_WIDTH),
        ],
        out_shape=[
            jax.ShapeDtypeStruct((batch, N_HEADS, HEAD_DIM, seq), BF16),
            jax.ShapeDtypeStruct((batch, N_KV_HEADS, seq, HEAD_DIM), BF16),
            jax.ShapeDtypeStruct((batch, N_KV_HEADS, V_ROWS, seq), BF16),
            jax.ShapeDtypeStruct((batch, seq, ATTN_WIDTH), BF16),
            jax.ShapeDtypeStruct((batch, seq, 2 * F_WIDTH), BF16),
            jax.ShapeDtypeStruct((batch, seq, F_WIDTH), BF16),
        ],
        compiler_params=_params("parallel", "parallel"),
        name="proj_in",
    )(x, w["w_in"], w["g_norm"], w["g_q"], w["g_k"], w["cos"], w["sin"], w["bd"], w["wz"])

    kv_width = Q_PER_KV * HEAD_DIM
    a = pl.pallas_call(
        functools.partial(_attn_kernel, q_tile=Q_TILE, k_tile=K_TILE, q_chunk=Q_CHUNK),
        grid=(batch, N_KV_HEADS),
        in_specs=[
            pl.BlockSpec((1, Q_PER_KV, HEAD_DIM, seq), lambda b, g: (b, g, 0, 0)),
            pl.BlockSpec((1, 1, seq, HEAD_DIM), lambda b, g: (b, g, 0, 0)),
            pl.BlockSpec((1, 1, V_ROWS, seq), lambda b, g: (b, g, 0, 0)),
        ],
        out_specs=pl.BlockSpec((1, seq, kv_width), lambda b, g: (b, 0, g)),
        out_shape=jax.ShapeDtypeStruct((batch, seq, ATTN_WIDTH), BF16),
        scratch_shapes=[pltpu.VMEM((1, Q_PER_KV * seq), F32),
                        pltpu.VMEM((V_ROWS, Q_PER_KV * seq), F32),
                        pltpu.VMEM((Q_PER_KV * Q_TILE // Q_CHUNK, K_TILE, Q_CHUNK), F32),
                        pltpu.VMEM((Q_PER_KV * Q_TILE // Q_CHUNK, K_TILE, Q_CHUNK), F32),
                        pltpu.VMEM((Q_PER_KV * Q_TILE // Q_CHUNK, 1, Q_CHUNK), F32),
                        pltpu.VMEM((Q_PER_KV * Q_TILE // Q_CHUNK, 1, Q_CHUNK), F32)],
        compiler_params=_params("parallel", "parallel"),
        name="attn",
    )(q, k, vt)

    half = seq // 2
    resident = lambda shape: pl.BlockSpec(shape, lambda b: (0,) * len(shape),
                                          pipeline_mode=pl.Buffered(1))
    fm = pl.pallas_call(
        _dft_kernel,
        grid=(batch,),
        in_specs=[
            resident((half, half)),
            resident((half, half)),
            resident(w["alt"].shape),
            resident((REV_BLOCK, REV_BLOCK)),
            pl.BlockSpec((1, seq, 2 * F_WIDTH), lambda b: (b, 0, 0)),
        ],
        out_specs=pl.BlockSpec((1, seq, F_WIDTH), lambda b: (b, 0, 0)),
        out_shape=jax.ShapeDtypeStruct((batch, seq, F_WIDTH), BF16),
        scratch_shapes=[pltpu.VMEM((half, 2 * F_WIDTH), BF16),
                        pltpu.VMEM((half, F_WIDTH), BF16)],
        compiler_params=_params("parallel"),
        name="dft",
    )(w["ct"], w["st"], w["alt"], w["rev"], xcs)

    ple_dim = p.shape[-1]
    return pl.pallas_call(
        functools.partial(_proj_out_kernel, final_norm=final_norm),
        grid=(batch, nt),
        in_specs=[
            tok(d_model), tok(ATTN_WIDTH), tok(ATTN_WIDTH), tok(F_WIDTH), tok(F_WIDTH),
            tok(ple_dim),
            _const_spec((ATTN_WIDTH + F_WIDTH, d_model)),
            _const_spec((d_model, d_model)),
            _const_spec((ple_dim, d_model)),
            _const_spec((1, d_model)),
            _const_spec((1, d_model)),
        ],
        out_specs=tok(d_model),
        out_shape=jax.ShapeDtypeStruct((batch, seq, d_model), F32),
        compiler_params=_params("parallel", "parallel"),
        name="proj_out",
    )(x, a, ga, fm, gf, p, w["w_out"], w["w_ple_gate"], w["w_ple"], w["g_ple"], w["g_final"])


def _layer_weights(i, seq, g_norm, w_in, g_q, g_k, w_fmix, w_out, g_ple, w_ple_gate, w_ple, g_final):
    cos, sin = _rope_tables(seq)
    ct, st, alt = _seq_dft_tables(seq)
    heads = np.arange(ATTN_WIDTH) // HEAD_DIM
    bd = jnp.asarray((heads[:, None] == heads[None, :]).astype(np.float32) / HEAD_DIM, dtype=BF16)
    return {
        "w_in": w_in[i].astype(BF16),
        "g_norm": g_norm[i][None, :],
        "g_q": jnp.tile(g_q[i], N_HEADS)[None, :],
        "g_k": jnp.tile(g_k[i], N_KV_HEADS)[None, :],
        "cos": cos, "sin": sin, "bd": bd,
        "wz": _fourier_weights(w_fmix[i], seq),
        "ct": ct, "st": st, "alt": alt, "rev": _shifted_reversal(REV_BLOCK),
        "w_out": w_out[i].astype(BF16),
        "w_ple_gate": w_ple_gate[i].astype(BF16),
        "w_ple": w_ple[i].astype(BF16),
        "g_ple": g_ple[i][None, :],
        "g_final": g_final[None, :],
    }


def _trunk(x, p, layer_weights):
    h = x
    depth = len(layer_weights)
    for i, w in enumerate(layer_weights):
        h = _layer(h, p[i], w, final_norm=(i == depth - 1))
    return h


def kernel(x_prompt, x_sample, p_prompt, p_sample, g_norm, w_in, g_q, g_k, w_fmix, w_out,
           g_ple, w_ple_gate, w_ple, g_final):
    depth = g_norm.shape[0]
    outs = []
    weights_by_seq = {}
    for x, p in ((x_prompt, p_prompt), (x_sample, p_sample)):
        seq = x.shape[1]
        if seq not in weights_by_seq:
            weights_by_seq[seq] = [
                _layer_weights(i, seq, g_norm, w_in, g_q, g_k, w_fmix, w_out, g_ple,
                               w_ple_gate, w_ple, g_final) for i in range(depth)]
        outs.append(_trunk(x, p, weights_by_seq[seq]))
    return tuple(outs)
```
